```python
import jax, jax.numpy as jnp
from jax import lax
import numpy as np

D_MODEL = 4096
BATCH = 4
SEQ = 4096
DEPTH = 2

GRID_W = 64
CTX_LEN = 256
N_MIXERS = 2
HEAD_DIM = 128
N_Q_HEADS = D_MODEL // HEAD_DIM
N_KV_HEADS = N_Q_HEADS // 4
Q_BLOCK = 128
D_FF = 4 * D_MODEL
CONV_WIDTH = 3
ROPE_THETA = 10000.0
LN_EPS = 1e-5
RMS_EPS = 1e-6
N_MOD = 6
N_CONV_LAYERS = (DEPTH + 1) // 2
N_ATTN_LAYERS = DEPTH // 2
DEEPNORM_ALPHA = (2 * DEPTH) ** 0.25
DEEPNORM_BETA = (8 * DEPTH) ** -0.25

kernel_name = "hybrid_conv_gqa_deepnorm_dit"


def layer_norm(x, g, b):
    xf = x.astype(jnp.float32)
    mu = jnp.mean(xf, axis=-1, keepdims=True)
    var = jnp.mean(jnp.square(xf - mu), axis=-1, keepdims=True)
    y = (xf - mu) * lax.rsqrt(var + LN_EPS) * g.astype(jnp.float32) + b.astype(jnp.float32)
    return y.astype(x.dtype)


def rms_norm(x, g):
    xf = x.astype(jnp.float32)
    y = xf * lax.rsqrt(jnp.mean(xf * xf, axis=-1, keepdims=True) + RMS_EPS) * g.astype(jnp.float32)
    return y.astype(x.dtype)


def adaln(cond, w, b):
    m = jnp.dot(jax.nn.silu(cond), w) + b
    m = m.reshape(-1, 1, N_MOD * D_MODEL)
    return jnp.split(m, N_MOD, axis=-1)


def modulate(h, shift, scale):
    return h * (1.0 + scale) + shift


def axial_rope_tables(seq):
    rows = seq // GRID_W
    row = jnp.repeat(jnp.arange(rows), GRID_W)
    col = jnp.tile(jnp.arange(GRID_W), rows)
    half = HEAD_DIM // 2
    inv_freq = ROPE_THETA ** (-jnp.arange(0, half, 2, dtype=jnp.float32) / half)

    def axis_angles(pos):
        ang = pos.astype(jnp.float32)[:, None] * inv_freq[None, :]
        return jnp.concatenate([ang, ang], axis=-1)

    ang = jnp.concatenate([axis_angles(row), axis_angles(col)], axis=-1)
    return jnp.cos(ang), jnp.sin(ang)


def apply_axial_rope(x, cos, sin):
    xf = x.astype(jnp.float32)
    quarter = HEAD_DIM // 4
    xa = xf.reshape(*xf.shape[:-1], 2, 2, quarter)
    rot = jnp.stack([-xa[..., 1, :], xa[..., 0, :]], axis=-2).reshape(xf.shape)
    return (xf * cos[None, :, None, :] + rot * sin[None, :, None, :]).astype(x.dtype)


def short_conv_mixer(h, w_in, conv_w, conv_b, w_out):
    gate_b, gate_c, v = jnp.split(jnp.dot(h, w_in), 3, axis=-1)
    u = gate_c * v
    length = u.shape[1]
    up = jnp.pad(u, ((0, 0), (1, 1), (0, 0)))
    conv = (up[:, :length] * conv_w[0] + up[:, 1:length + 1] * conv_w[1]
            + up[:, 2:length + 2] * conv_w[2] + conv_b)
    return jnp.dot(gate_b * conv, w_out)


def attend(q, k, v):
    s = jnp.einsum('bqkgd,bskd->bkgqs', q, k, preferred_element_type=jnp.float32) * (HEAD_DIM ** -0.5)
    p = jax.nn.softmax(s, axis=-1).astype(v.dtype)
    return jnp.einsum('bkgqs,bskd->bqkgd', p, v)


def gqa_axial_attention(h_lat, h_ctx, w_qkv, q_gain, k_gain, w_o, cos, sin, need_ctx_out):
    bsz, seq, _ = h_lat.shape
    ctx_len = h_ctx.shape[1]
    grp = N_Q_HEADS // N_KV_HEADS
    q_dim = N_Q_HEADS * HEAD_DIM
    kv_dim = N_KV_HEADS * HEAD_DIM

    def split_kv(kv, length):
        k, v = jnp.split(kv, 2, axis=-1)
        k = rms_norm(k.reshape(bsz, length, N_KV_HEADS, HEAD_DIM), k_gain)
        return k, v.reshape(bsz, length, N_KV_HEADS, HEAD_DIM)

    qkv_l = jnp.dot(h_lat, w_qkv)
    q_l = rms_norm(qkv_l[..., :q_dim].reshape(bsz, seq, N_Q_HEADS, HEAD_DIM), q_gain)
    k_l, v_l = split_kv(qkv_l[..., q_dim:], seq)
    q_l = apply_axial_rope(q_l, cos, sin)
    k_l = apply_axial_rope(k_l, cos, sin)

    k_c, v_c = split_kv(jnp.dot(h_ctx, w_qkv[:, q_dim:q_dim + 2 * kv_dim]), ctx_len)

    k_all = jnp.concatenate([k_c, k_l], axis=1)
    v_all = jnp.concatenate([v_c, v_l], axis=1)
    n_blk = seq // Q_BLOCK
    q_blocks = q_l.reshape(bsz, n_blk, Q_BLOCK, N_KV_HEADS, grp, HEAD_DIM).transpose(1, 0, 2, 3, 4, 5)
    o = lax.map(lambda qb: attend(qb, k_all, v_all), q_blocks)
    o_lat = o.transpose(1, 0, 2, 3, 4, 5).reshape(bsz, seq, q_dim)
    y_lat = jnp.dot(o_lat, w_o)

    y_ctx = None
    if need_ctx_out:
        q_c = rms_norm(jnp.dot(h_ctx, w_qkv[:, :q_dim]).reshape(bsz, ctx_len, N_Q_HEADS, HEAD_DIM), q_gain)
        o_ctx = attend(q_c.reshape(bsz, ctx_len, N_KV_HEADS, grp, HEAD_DIM), k_c, v_c)
        y_ctx = jnp.dot(o_ctx.reshape(bsz, ctx_len, q_dim), w_o)
    return y_lat, y_ctx


def sqrelu_mlp(h, w1, w2):
    return jnp.dot(jnp.square(jax.nn.relu(jnp.dot(h, w1))), w2)


def setup_inputs(seed: int = 0) -> dict:
    key = jax.random.key(seed)
    ks = jax.random.split(key, 20)
    d = D_MODEL
    q_dim = N_Q_HEADS * HEAD_DIM
    kv_dim = N_KV_HEADS * HEAD_DIM

    def nrm(k, shape, s):
        return jax.random.normal(k, shape, jnp.float32) * s

    w_qk = nrm(ks[14], (N_ATTN_LAYERS, d, q_dim + kv_dim), d ** -0.5)
    w_v = nrm(ks[15], (N_ATTN_LAYERS, d, kv_dim), DEEPNORM_BETA * d ** -0.5)
    return {
        "x": nrm(ks[0], (BATCH, SEQ, d), 1.0),
        "c": nrm(ks[1], (BATCH, d), 1.0),
        "ctx": nrm(ks[2], (BATCH, CTX_LEN, d), 1.0),
        "c_ctx": nrm(ks[3], (d,), 1.0),
        "ada_w": nrm(ks[4], (DEPTH, d, N_MOD * d), 0.5 * d ** -0.5),
        "ada_b": nrm(ks[5], (DEPTH, N_MOD * d), 0.02),
        "ln_g": 1.0 + nrm(ks[6], (DEPTH, 2, d), 0.02),
        "ln_b": nrm(ks[7], (DEPTH, 2, d), 0.02),
        "mlp_w1": nrm(ks[8], (DEPTH, d, D_FF), d ** -0.5),
        "mlp_w2": nrm(ks[9], (DEPTH, D_FF, d), DEEPNORM_BETA * D_FF ** -0.5),
        "conv_in_w": nrm(ks[10], (N_CONV_LAYERS, d, 3 * d), d ** -0.5),
        "conv_w": nrm(ks[11], (N_CONV_LAYERS, CONV_WIDTH, d), CONV_WIDTH ** -0.5),
        "conv_b": nrm(ks[12], (N_CONV_LAYERS, d), 0.02),
        "conv_out_w": nrm(ks[13], (N_CONV_LAYERS, d, d), DEEPNORM_BETA * d ** -0.5),
        "attn_qkv_w": jnp.concatenate([w_qk, w_v], axis=-1),
        "attn_q_gain": 1.0 + nrm(ks[16], (N_ATTN_LAYERS, HEAD_DIM), 0.02),
        "attn_k_gain": 1.0 + nrm(ks[17], (N_ATTN_LAYERS, HEAD_DIM), 0.02),
        "attn_out_w": nrm(ks[18], (N_ATTN_LAYERS, q_dim, d), DEEPNORM_BETA * q_dim ** -0.5),
    }


def reference(x, c, ctx, c_ctx, ada_w, ada_b, ln_g, ln_b, mlp_w1, mlp_w2, conv_in_w, conv_w, conv_b,
              conv_out_w, attn_qkv_w, attn_q_gain, attn_k_gain, attn_out_w):
    cos, sin = axial_rope_tables(x.shape[1])
    h_lat, h_ctx = x, ctx
    for i in range(DEPTH):
        need_ctx_out = i < DEPTH - 1
        j = i // N_MIXERS
        ml = adaln(c, ada_w[i], ada_b[i])
        mc = adaln(c_ctx, ada_w[i], ada_b[i])

        a_lat = modulate(h_lat, ml[0], ml[1])
        a_ctx = modulate(h_ctx, mc[0], mc[1])
        if i % N_MIXERS == 0:
            y_lat = short_conv_mixer(a_lat, conv_in_w[j], conv_w[j], conv_b[j], conv_out_w[j])
            y_ctx = (short_conv_mixer(a_ctx, conv_in_w[j], conv_w[j], conv_b[j], conv_out_w[j])
                     if need_ctx_out else None)
        else:
            y_lat, y_ctx = gqa_axial_attention(a_lat, a_ctx, attn_qkv_w[j], attn_q_gain[j], attn_k_gain[j],
                                               attn_out_w[j], cos, sin, need_ctx_out)
        h_lat = layer_norm(DEEPNORM_ALPHA * h_lat + ml[2] * y_lat, ln_g[i, 0], ln_b[i, 0])
        if need_ctx_out:
            h_ctx = layer_norm(DEEPNORM_ALPHA * h_ctx + mc[2] * y_ctx, ln_g[i, 0], ln_b[i, 0])

        y_lat = sqrelu_mlp(modulate(h_lat, ml[3], ml[4]), mlp_w1[i], mlp_w2[i])
        h_lat = layer_norm(DEEPNORM_ALPHA * h_lat + ml[5] * y_lat, ln_g[i, 1], ln_b[i, 1])
        if need_ctx_out:
            y_ctx = sqrelu_mlp(modulate(h_ctx, mc[3], mc[4]), mlp_w1[i], mlp_w2[i])
            h_ctx = layer_norm(DEEPNORM_ALPHA * h_ctx + mc[5] * y_ctx, ln_g[i, 1], ln_b[i, 1])
    return h_lat
```

```python
import functools
from typing import Callable, NamedTuple

import jax
import jax.numpy as jnp
from jax import lax
from jax.experimental import pallas as pl
from jax.experimental.pallas import tpu as pltpu

F32 = jnp.float32
BF16 = jnp.bfloat16

HEAD_DIM = 128
GQA_GROUP = 4
GRID_W = 64
N_MOD = 6
ROPE_THETA = 10000.0
LN_EPS = 1e-5
RMS_EPS = 1e-6
COND_ROWS = 8
BF16_SUBLANES = 16
VMEM_LIMIT = 60 * 1024 * 1024


class _RowMap(NamedTuple):
    tile_limit: int
    mod_row: Callable


def _cparams(sem):
    return pltpu.CompilerParams(dimension_semantics=sem, vmem_limit_bytes=VMEM_LIMIT)


def _pick(pref, n):
    t = min(pref, n)
    while n % t:
        t //= 2
    return t


def _dot(a, b):
    return jnp.dot(a, b, preferred_element_type=F32)


def _adaln_kernel(cond_ref, w_ref, b_ref, o_ref):
    c = cond_ref[...]
    s = (c * jax.nn.sigmoid(c)).astype(BF16)
    o_ref[...] = _dot(s, w_ref[...].astype(BF16)) + b_ref[...]


def _adaln(cond, ada_w, ada_b):
    depth, d, n = ada_w.shape
    tn = _pick(512, n)
    return pl.pallas_call(
        _adaln_kernel,
        out_shape=jax.ShapeDtypeStruct((depth, COND_ROWS, n), F32),
        grid=(depth, n // tn),
        in_specs=[
            pl.BlockSpec((COND_ROWS, d), lambda l, j: (0, 0)),
            pl.BlockSpec((None, d, tn), lambda l, j: (l, 0, j)),
            pl.BlockSpec((None, 1, tn), lambda l, j: (l, 0, j)),
        ],
        out_specs=pl.BlockSpec((None, COND_ROWS, tn), lambda l, j: (l, 0, j)),
        compiler_params=_cparams(("arbitrary", "arbitrary")),
        name="adaln",
    )(cond, ada_w, ada_b.reshape(depth, 1, n))


def _mod_spec(layer, which, row_of_tile, d):
    return pl.BlockSpec((None, None, None, 1, d),
                        lambda i, *_: (layer, row_of_tile(i), which, 0, 0))


def _row_spec(d):
    return pl.BlockSpec((1, d), lambda *_: (0, 0))


def _modulate_kernel(x_ref, shift_ref, scale_ref, o_ref):
    o_ref[...] = (x_ref[...] * (1.0 + scale_ref[...]) + shift_ref[...]).astype(BF16)


def _modulate(x, mods, layer, row_of_tile_fn, tm):
    rows, d = x.shape
    rot = row_of_tile_fn.mod_row(tm)
    return pl.pallas_call(
        _modulate_kernel,
        out_shape=jax.ShapeDtypeStruct((rows, d), BF16),
        grid=(rows // tm,),
        in_specs=[
            pl.BlockSpec((tm, d), lambda i: (i, 0)),
            _mod_spec(layer, 0, rot, d),
            _mod_spec(layer, 1, rot, d),
        ],
        out_specs=pl.BlockSpec((tm, d), lambda i: (i, 0)),
        compiler_params=_cparams(("arbitrary",)),
        name="modulate",
    )(x, mods, mods)


def _conv_in_kernel(a_ref, wb_ref, wc_ref, wv_ref, gb_ref, u_ref, wsb, wsc, wsv):
    @pl.when(pl.program_id(1) == 0)
    def _():
        wsb[...] = wb_ref[...].astype(BF16)
        wsc[...] = wc_ref[...].astype(BF16)
        wsv[...] = wv_ref[...].astype(BF16)

    a = a_ref[...]
    gb_ref[...] = _dot(a, wsb[...]).astype(BF16)
    u_ref[...] = (_dot(a, wsc[...]) * _dot(a, wsv[...])).astype(BF16)


def _conv_in(a, w_in, tile_limit):
    rows, d = a.shape
    tm = _pick(1024, tile_limit)
    tn = _pick(256, d)
    nj = d // tn
    out = jax.ShapeDtypeStruct((rows, d), BF16)
    w_spec = lambda t: pl.BlockSpec((d, tn), lambda j, i: (0, j + t * nj))
    return pl.pallas_call(
        _conv_in_kernel,
        out_shape=(out, out),
        grid=(nj, rows // tm),
        in_specs=[pl.BlockSpec((tm, d), lambda j, i: (i, 0)), w_spec(0), w_spec(1), w_spec(2)],
        out_specs=(pl.BlockSpec((tm, tn), lambda j, i: (i, j)),
                   pl.BlockSpec((tm, tn), lambda j, i: (i, j))),
        scratch_shapes=[pltpu.VMEM((d, tn), BF16)] * 3,
        compiler_params=_cparams(("arbitrary", "arbitrary")),
        name="conv_in",
    )(a, w_in, w_in, w_in)


LN_ROW_CHUNK = 32
ACC_COL_CHUNK = 1024


def _accumulate(o_ref, lhs, w_ref, gate_ref, hres_ref, alpha, step, n_steps):
    tm, d = o_ref.shape

    @pl.when(step == 0)
    def _():
        o_ref[...] = jnp.zeros_like(o_ref)

    cw = _pick(ACC_COL_CHUNK, d)
    for n in range(d // cw):
        sl = slice(n * cw, (n + 1) * cw)
        o_ref[:, sl] += gate_ref[:, sl] * _dot(lhs, w_ref[:, sl])
    rc = tm // n_steps
    r0 = pl.multiple_of(step * rc, rc)
    o_ref[pl.ds(r0, rc), :] += alpha * hres_ref[...]


def _layer_norm_rows(o_ref, g_ref, b_ref, an_ref, nshift_ref, nscale_ref):
    tm, d = o_ref.shape
    rb = _pick(LN_ROW_CHUNK, tm)

    def body(r, carry):
        r0 = pl.multiple_of(r * rb, rb)
        z = o_ref[pl.ds(r0, rb), :]
        mu = jnp.mean(z, axis=-1, keepdims=True)
        zc = z - mu
        var = jnp.mean(zc * zc, axis=-1, keepdims=True)
        y = zc * lax.rsqrt(var + LN_EPS) * g_ref[...] + b_ref[...]
        o_ref[pl.ds(r0, rb), :] = y
        if an_ref is not None:
            an_ref[pl.ds(r0, rb), :] = (y * (1.0 + nscale_ref[...]) + nshift_ref[...]).astype(BF16)
        return carry

    lax.fori_loop(0, tm // rb, body, 0)


def _split_tail(refs, has_next):
    if has_next:
        *head, nshift_ref, nscale_ref, o_ref, an_ref = refs
    else:
        *head, o_ref = refs
        nshift_ref = nscale_ref = an_ref = None
    return head, nshift_ref, nscale_ref, o_ref, an_ref


def _mlp_kernel(*refs, alpha, n_steps, has_next):
    head, nshift_ref, nscale_ref, o_ref, an_ref = _split_tail(refs, has_next)
    a_ref, w1_ref, w2_ref, hres_ref, gate_ref, g_ref, b_ref = head
    step = pl.program_id(1)
    hid = jnp.maximum(_dot(a_ref[...], w1_ref[...]), 0.0)
    hid = (hid * hid).astype(BF16)
    _accumulate(o_ref, hid, w2_ref, gate_ref, hres_ref, alpha, step, n_steps)

    @pl.when(step == n_steps - 1)
    def _():
        _layer_norm_rows(o_ref, g_ref, b_ref, an_ref, nshift_ref, nscale_ref)


def _plain_out_kernel(*refs, alpha, n_steps, has_next):
    head, nshift_ref, nscale_ref, o_ref, an_ref = _split_tail(refs, has_next)
    lhs_ref, w_ref, hres_ref, gate_ref, g_ref, b_ref = head
    step = pl.program_id(1)
    _accumulate(o_ref, lhs_ref[...], w_ref, gate_ref, hres_ref, alpha, step, n_steps)

    @pl.when(step == n_steps - 1)
    def _():
        _layer_norm_rows(o_ref, g_ref, b_ref, an_ref, nshift_ref, nscale_ref)


def _conv_out_kernel(*refs, alpha, n_steps, has_next, seq_len):
    head, nshift_ref, nscale_ref, o_ref, an_ref = _split_tail(refs, has_next)
    gb_ref, u_ref, up_ref, un_ref, cw_ref, cb_ref, w_ref, hres_ref, gate_ref, g_ref, b_ref = head
    step = pl.program_id(1)
    tm = u_ref.shape[0]
    u = u_ref[...].astype(F32)
    local = lax.broadcasted_iota(jnp.int32, (tm, 1), 0)
    pos = (local + pl.program_id(0) * tm) % seq_len
    prev_row = up_ref[BF16_SUBLANES - 1:BF16_SUBLANES, :].astype(F32)
    next_row = un_ref[0:1, :].astype(F32)
    u_prev = jnp.where(local == 0, prev_row, pltpu.roll(u, 1, axis=0))
    u_prev = jnp.where(pos == 0, 0.0, u_prev)
    u_next = jnp.where(local == tm - 1, next_row, pltpu.roll(u, tm - 1, axis=0))
    u_next = jnp.where(pos == seq_len - 1, 0.0, u_next)
    conv = u_prev * cw_ref[0:1, :] + u * cw_ref[1:2, :] + u_next * cw_ref[2:3, :] + cb_ref[...]
    lhs = (gb_ref[...].astype(F32) * conv).astype(BF16)
    _accumulate(o_ref, lhs, w_ref, gate_ref, hres_ref, alpha, step, n_steps)

    @pl.when(step == n_steps - 1)
    def _():
        _layer_norm_rows(o_ref, g_ref, b_ref, an_ref, nshift_ref, nscale_ref)


def _residual_ln_call(kernel, name, lead_specs, lead_args, hres, mods, layer, gate_idx, rot_fn,
                      ln_g, ln_b, next_mod, tm, n_steps, alpha, **kernel_kwargs):
    rows, d = hres.shape
    rc = tm // n_steps
    assert rc * n_steps == tm and rc % 8 == 0, (tm, n_steps)
    rot = rot_fn.mod_row(tm)
    in_specs = list(lead_specs) + [
        pl.BlockSpec((rc, d), lambda i, k: (i * n_steps + k, 0)),
        _mod_spec(layer, gate_idx, rot, d),
        _row_spec(d),
        _row_spec(d),
    ]
    args = list(lead_args) + [hres, mods, ln_g, ln_b]
    out_shape = [jax.ShapeDtypeStruct((rows, d), F32)]
    out_specs = [pl.BlockSpec((tm, d), lambda i, k: (i, 0))]
    has_next = next_mod is not None
    if has_next:
        nl, nidx = next_mod
        in_specs += [_mod_spec(nl, nidx, rot, d), _mod_spec(nl, nidx + 1, rot, d)]
        args += [mods, mods]
        out_shape.append(jax.ShapeDtypeStruct((rows, d), BF16))
        out_specs.append(pl.BlockSpec((tm, d), lambda i, k: (i, 0)))
    res = pl.pallas_call(
        functools.partial(kernel, alpha=alpha, n_steps=n_steps, has_next=has_next, **kernel_kwargs),
        out_shape=tuple(out_shape),
        grid=(rows // tm, n_steps),
        in_specs=in_specs,
        out_specs=tuple(out_specs),
        compiler_params=_cparams(("arbitrary", "arbitrary")),
        name=name,
    )(*args)
    return res if has_next else (res[0], None)


def _mlp(a, w1, w2, hres, mods, layer, rot_fn, ln_g, ln_b, next_mod, alpha):
    rows, d = a.shape
    d_ff = w1.shape[1]
    tm = _pick(512, rot_fn.tile_limit)
    tf = _pick(512, d_ff)
    lead_specs = [
        pl.BlockSpec((tm, d), lambda i, k: (i, 0)),
        pl.BlockSpec((d, tf), lambda i, k: (0, k)),
        pl.BlockSpec((tf, d), lambda i, k: (k, 0)),
    ]
    return _residual_ln_call(_mlp_kernel, "mlp", lead_specs, [a, w1, w2], hres, mods, layer, 5, rot_fn,
                             ln_g, ln_b, next_mod, tm, d_ff // tf, alpha)


def _plain_out(lhs, w, hres, mods, layer, rot_fn, ln_g, ln_b, next_mod, alpha):
    rows, kdim = lhs.shape
    d = w.shape[1]
    tm = _pick(512, rot_fn.tile_limit)
    tk = _pick(512, kdim)
    lead_specs = [
        pl.BlockSpec((tm, tk), lambda i, k: (i, k)),
        pl.BlockSpec((tk, d), lambda i, k: (k, 0)),
    ]
    return _residual_ln_call(_plain_out_kernel, "attn_out", lead_specs, [lhs, w], hres, mods, layer, 2,
                             rot_fn, ln_g, ln_b, next_mod, tm, kdim // tk, alpha)


def _conv_out(gb, u, conv_w, conv_b, w, hres, mods, layer, rot_fn, ln_g, ln_b, next_mod, alpha, seq_len):
    rows, kdim = u.shape
    d = w.shape[1]
    tm = _pick(512, rot_fn.tile_limit)
    tk = _pick(512, kdim)
    hb = BF16_SUBLANES
    n_halo = rows // hb
    lead_specs = [
        pl.BlockSpec((tm, tk), lambda i, k: (i, k)),
        pl.BlockSpec((tm, tk), lambda i, k: (i, k)),
        pl.BlockSpec((hb, tk), lambda i, k: (jnp.maximum(i * (tm // hb) - 1, 0), k)),
        pl.BlockSpec((hb, tk), lambda i, k: (jnp.minimum((i + 1) * (tm // hb), n_halo - 1), k)),
        pl.BlockSpec((3, tk), lambda i, k: (0, k)),
        pl.BlockSpec((1, tk), lambda i, k: (0, k)),
        pl.BlockSpec((tk, d), lambda i, k: (k, 0)),
    ]
    return _residual_ln_call(_conv_out_kernel, "conv_out", lead_specs, [gb, u, u, u, conv_w, conv_b, w],
                             hres, mods, layer, 2, rot_fn, ln_g, ln_b, next_mod, tm, kdim // tk, alpha,
                             seq_len=seq_len)


def _rms_rope_heads(y, gain_ref, rope_refs, o_ref):
    tm, tn = y.shape
    if rope_refs is not None:
        cos_ref, sin_ref = rope_refs
        lane = lax.broadcasted_iota(jnp.int32, (tm, HEAD_DIM), 1)
        first_half = (lane % (HEAD_DIM // 2)) < (HEAD_DIM // 4)
    for h in range(tn // HEAD_DIM):
        sl = slice(h * HEAD_DIM, (h + 1) * HEAD_DIM)
        x = y[:, sl]
        x = x * lax.rsqrt(jnp.mean(x * x, axis=-1, keepdims=True) + RMS_EPS) * gain_ref[...]
        if rope_refs is not None:
            rot = jnp.where(first_half,
                            pltpu.roll(x, HEAD_DIM - HEAD_DIM // 4, axis=1),
                            pltpu.roll(x, HEAD_DIM // 4, axis=1))
            x = x * cos_ref[...] + rot * sin_ref[...]
        o_ref[:, sl] = x.astype(BF16)


def _qkv_kernel(*refs, n_q_tiles, n_k_tiles, rope):
    if rope:
        a_ref, w_ref, qg_ref, kg_ref, cos_ref, sin_ref, o_ref, ws = refs
        rope_refs = (cos_ref, sin_ref)
    else:
        a_ref, w_ref, qg_ref, kg_ref, o_ref, ws = refs
        rope_refs = None
    j = pl.program_id(0)

    @pl.when(pl.program_id(1) == 0)
    def _():
        ws[...] = w_ref[...].astype(BF16)

    y = _dot(a_ref[...], ws[...])

    if n_q_tiles:
        @pl.when(j < n_q_tiles)
        def _():
            _rms_rope_heads(y, qg_ref, rope_refs, o_ref)

    @pl.when((j >= n_q_tiles) & (j < n_q_tiles + n_k_tiles))
    def _():
        _rms_rope_heads(y, kg_ref, rope_refs, o_ref)

    @pl.when(j >= n_q_tiles + n_k_tiles)
    def _():
        o_ref[...] = y.astype(BF16)


def _qkv(a, w_qkv, q_gain, k_gain, rope_tables, col0, q_dim, kv_dim, seq_len):
    rows, d = a.shape
    n_cols = w_qkv.shape[1] - col0
    tm = _pick(1024, seq_len if rope_tables is not None else rows)
    tn = _pick(512, kv_dim)
    assert col0 % tn == 0 and q_dim % tn == 0
    j0 = col0 // tn
    n_q_tiles = (q_dim - col0) // tn if col0 < q_dim else 0
    n_k_tiles = kv_dim // tn
    rope = rope_tables is not None
    in_specs = [
        pl.BlockSpec((tm, d), lambda j, i: (i, 0)),
        pl.BlockSpec((d, tn), lambda j, i: (0, j + j0)),
        pl.BlockSpec((1, HEAD_DIM), lambda j, i: (0, 0)),
        pl.BlockSpec((1, HEAD_DIM), lambda j, i: (0, 0)),
    ]
    args = [a, w_qkv, q_gain, k_gain]
    if rope:
        tiles_per_seq = seq_len // tm
        tab = pl.BlockSpec((tm, HEAD_DIM), lambda j, i: (i % tiles_per_seq, 0))
        in_specs += [tab, tab]
        args += list(rope_tables)
    return pl.pallas_call(
        functools.partial(_qkv_kernel, n_q_tiles=n_q_tiles, n_k_tiles=n_k_tiles, rope=rope),
        out_shape=jax.ShapeDtypeStruct((rows, n_cols), BF16),
        grid=(n_cols // tn, rows // tm),
        in_specs=in_specs,
        out_specs=pl.BlockSpec((tm, tn), lambda j, i: (i, j)),
        scratch_shapes=[pltpu.VMEM((d, tn), BF16)],
        compiler_params=_cparams(("arbitrary", "arbitrary")),
        name="qkv" if rope else "kv_ctx",
    )(*args)


def _rope_tables(seq_len):
    rows = seq_len // GRID_W
    row = jnp.repeat(jnp.arange(rows), GRID_W)
    col = jnp.tile(jnp.arange(GRID_W), rows)
    half = HEAD_DIM // 2
    inv_freq = ROPE_THETA ** (-jnp.arange(0, half, 2, dtype=F32) / half)

    def axis_angles(pos):
        ang = pos.astype(F32)[:, None] * inv_freq[None, :]
        return jnp.concatenate([ang, ang], axis=-1)

    ang = jnp.concatenate([axis_angles(row), axis_angles(col)], axis=-1)
    lane = jnp.arange(HEAD_DIM)
    sign = jnp.where((lane % half) < (HEAD_DIM // 4), -1.0, 1.0).astype(F32)
    return jnp.cos(ang), jnp.sin(ang) * sign[None, :]


_NT = (((1,), (1,)), ((), ()))


def _attention_kernel(q_ref, kc_ref, vc_ref, kl_ref, vl_ref, o_ref, *, scale):
    for g in range(GQA_GROUP):
        sl = slice(g * HEAD_DIM, (g + 1) * HEAD_DIM)
        q = q_ref[:, sl]
        s_c = lax.dot_general(q, kc_ref[...], _NT, preferred_element_type=F32) * scale
        s_l = lax.dot_general(q, kl_ref[...], _NT, preferred_element_type=F32) * scale
        m = jnp.maximum(jnp.max(s_c, axis=-1, keepdims=True), jnp.max(s_l, axis=-1, keepdims=True))
        p_c = jnp.exp(s_c - m)
        p_l = jnp.exp(s_l - m)
        denom = jnp.sum(p_c, axis=-1, keepdims=True) + jnp.sum(p_l, axis=-1, keepdims=True)
        o = _dot(p_c.astype(BF16), vc_ref[...]) + _dot(p_l.astype(BF16), vl_ref[...])
        o_ref[:, sl] = (o / denom).astype(BF16)


def _attention(qkv_lat, kv_ctx, bsz, seq_len, ctx_len, q_dim, kv_dim):
    n_kv = kv_dim // HEAD_DIM
    gw = GQA_GROUP * HEAD_DIM
    tq = _pick(256, seq_len)
    tiles = seq_len // tq
    k0 = q_dim // HEAD_DIM
    v0 = k0 + n_kv
    return pl.pallas_call(
        functools.partial(_attention_kernel, scale=HEAD_DIM ** -0.5),
        out_shape=jax.ShapeDtypeStruct((bsz * seq_len, q_dim), BF16),
        grid=(bsz, n_kv, tiles),
        in_specs=[
            pl.BlockSpec((tq, gw), lambda b, h, t: (b * tiles + t, h)),
            pl.BlockSpec((ctx_len, HEAD_DIM), lambda b, h, t: (b, h)),
            pl.BlockSpec((ctx_len, HEAD_DIM), lambda b, h, t: (b, n_kv + h)),
            pl.BlockSpec((seq_len, HEAD_DIM), lambda b, h, t: (b, k0 + h)),
            pl.BlockSpec((seq_len, HEAD_DIM), lambda b, h, t: (b, v0 + h)),
        ],
        out_specs=pl.BlockSpec((tq, gw), lambda b, h, t: (b * tiles + t, h)),
        compiler_params=_cparams(("arbitrary", "arbitrary", "arbitrary")),
        name="attention",
    )(qkv_lat, kv_ctx, kv_ctx, qkv_lat, qkv_lat)


def kernel(x, c, ctx, c_ctx, ada_w, ada_b, ln_g, ln_b, mlp_w1, mlp_w2, conv_in_w, conv_w, conv_b, conv_out_w,
           attn_qkv_w, attn_q_gain, attn_k_gain, attn_out_w):
    bsz, seq_len, d = x.shape
    ctx_len = ctx.shape[1]
    depth = ada_w.shape[0]
    assert depth == 2 and bsz < COND_ROWS, "one conv-mixer layer followed by one attention layer"
    alpha = (2 * depth) ** 0.25
    kv_dim = (attn_qkv_w.shape[2] - d) // 2
    q_dim = d

    cond = jnp.zeros((COND_ROWS, d), F32).at[:bsz].set(c).at[bsz].set(c_ctx)
    mods = _adaln(cond, ada_w, ada_b).reshape(depth, COND_ROWS, N_MOD, 1, d)

    x_lat = x.reshape(bsz * seq_len, d)
    x_ctx = ctx.reshape(bsz * ctx_len, d)
    lat_rot = _RowMap(seq_len, lambda tm: (lambda i: i // (seq_len // tm)))
    ctx_rot = _RowMap(bsz * ctx_len, lambda tm: (lambda i: bsz))
    row = lambda v: v.reshape(1, -1)

    conv_out_wb = conv_out_w[0].astype(BF16)
    w1b = mlp_w1.astype(BF16)
    w2b = mlp_w2.astype(BF16)
    attn_out_wb = attn_out_w[0].astype(BF16)

    def layer0(h, rot_fn, sub_len, final_next_mod):
        a = _modulate(h, mods, 0, rot_fn, _pick(512, rot_fn.tile_limit))
        gb, u = _conv_in(a, conv_in_w[0], rot_fn.tile_limit)
        h1, a1 = _conv_out(gb, u, conv_w[0], row(conv_b[0]), conv_out_wb, h, mods, 0, rot_fn,
                           row(ln_g[0, 0]), row(ln_b[0, 0]), (0, 3), alpha, sub_len)
        return _mlp(a1, w1b[0], w2b[0], h1, mods, 0, rot_fn, row(ln_g[0, 1]), row(ln_b[0, 1]),
                    final_next_mod, alpha)

    h_lat, a_lat = layer0(x_lat, lat_rot, seq_len, (1, 0))
    _, a_ctx = layer0(x_ctx, ctx_rot, ctx_len, (1, 0))

    cos, sin_signed = _rope_tables(seq_len)
    qg, kg = row(attn_q_gain[0]), row(attn_k_gain[0])
    qkv_lat = _qkv(a_lat, attn_qkv_w[0], qg, kg, (cos, sin_signed), 0, q_dim, kv_dim, seq_len)
    kv_ctx = _qkv(a_ctx, attn_qkv_w[0], qg, kg, None, q_dim, q_dim, kv_dim, ctx_len)
    o = _attention(qkv_lat, kv_ctx, bsz, seq_len, ctx_len, q_dim, kv_dim)
    h_lat, a_lat = _plain_out(o, attn_out_wb, h_lat, mods, 1, lat_rot, row(ln_g[1, 0]), row(ln_b[1, 0]),
                              (1, 3), alpha)
    h_lat, _ = _mlp(a_lat, w1b[1], w2b[1], h_lat, mods, 1, lat_rot, row(ln_g[1, 1]), row(ln_b[1, 1]),
                    None, alpha)
    return h_lat.reshape(bsz, seq_len, d)
```

```python
import functools
from typing import Callable, NamedTuple

import jax
import jax.numpy as jnp
from jax import lax
from jax.experimental import pallas as pl
from jax.experimental.pallas import tpu as pltpu

F32 = jnp.float32
BF16 = jnp.bfloat16

HEAD_DIM = 128
GQA_GROUP = 4
GRID_W = 64
N_MOD = 6
ROPE_THETA = 10000.0
LN_EPS = 1e-5
RMS_EPS = 1e-6
LOG2_E = 1.4426950408889634
COND_ROWS = 8
BF16_SUBLANES = 16
VMEM_LIMIT = 60 * 1024 * 1024


class _RowMap(NamedTuple):
    tile_limit: int
    mod_row: Callable


def _cparams(sem):
    return pltpu.CompilerParams(dimension_semantics=sem, vmem_limit_bytes=VMEM_LIMIT)


def _pick(pref, n):
    t = min(pref, n)
    while n % t:
        t //= 2
    return t


def _dot(a, b):
    return jnp.dot(a, b, preferred_element_type=F32)


def _adaln_kernel(cond_ref, w_ref, b_ref, o_ref):
    c = cond_ref[...]
    s = (c * jax.nn.sigmoid(c)).astype(BF16)
    o_ref[...] = _dot(s, w_ref[...].astype(BF16)) + b_ref[...]


def _adaln(cond, ada_w, ada_b):
    depth, d, n = ada_w.shape
    tn = _pick(512, n)
    return pl.pallas_call(
        _adaln_kernel,
        out_shape=jax.ShapeDtypeStruct((depth, COND_ROWS, n), F32),
        grid=(depth, n // tn),
        in_specs=[
            pl.BlockSpec((COND_ROWS, d), lambda l, j: (0, 0)),
            pl.BlockSpec((None, d, tn), lambda l, j: (l, 0, j)),
            pl.BlockSpec((None, 1, tn), lambda l, j: (l, 0, j)),
        ],
        out_specs=pl.BlockSpec((None, COND_ROWS, tn), lambda l, j: (l, 0, j)),
        compiler_params=_cparams(("arbitrary", "arbitrary")),
        name="adaln",
    )(cond, ada_w, ada_b.reshape(depth, 1, n))


def _mod_spec(layer, which, row_of_tile, d):
    return pl.BlockSpec((None, None, None, 1, d),
                        lambda i, *_: (layer, row_of_tile(i), which, 0, 0))


def _row_spec(d):
    return pl.BlockSpec((1, d), lambda *_: (0, 0))


def _modulate_kernel(x_ref, shift_ref, scale_ref, o_ref):
    o_ref[...] = (x_ref[...] * (1.0 + scale_ref[...]) + shift_ref[...]).astype(BF16)


def _modulate(x, mods, layer, row_of_tile_fn, tm):
    rows, d = x.shape
    rot = row_of_tile_fn.mod_row(tm)
    return pl.pallas_call(
        _modulate_kernel,
        out_shape=jax.ShapeDtypeStruct((rows, d), BF16),
        grid=(rows // tm,),
        in_specs=[
            pl.BlockSpec((tm, d), lambda i: (i, 0)),
            _mod_spec(layer, 0, rot, d),
            _mod_spec(layer, 1, rot, d),
        ],
        out_specs=pl.BlockSpec((tm, d), lambda i: (i, 0)),
        compiler_params=_cparams(("arbitrary",)),
        name="modulate",
    )(x, mods, mods)


def _conv_in_kernel(a_ref, wb_ref, wc_ref, wv_ref, gb_ref, u_ref, wsb, wsc, wsv):
    @pl.when(pl.program_id(1) == 0)
    def _():
        wsb[...] = wb_ref[...].astype(BF16)
        wsc[...] = wc_ref[...].astype(BF16)
        wsv[...] = wv_ref[...].astype(BF16)

    a = a_ref[...]
    gb_ref[...] = _dot(a, wsb[...]).astype(BF16)
    u_ref[...] = (_dot(a, wsc[...]) * _dot(a, wsv[...])).astype(BF16)


def _conv_in(a, w_in, tile_limit):
    rows, d = a.shape
    tm = _pick(1024, tile_limit)
    tn = _pick(256, d)
    nj = d // tn
    out = jax.ShapeDtypeStruct((rows, d), BF16)
    w_spec = lambda t: pl.BlockSpec((d, tn), lambda j, i: (0, j + t * nj))
    return pl.pallas_call(
        _conv_in_kernel,
        out_shape=(out, out),
        grid=(nj, rows // tm),
        in_specs=[pl.BlockSpec((tm, d), lambda j, i: (i, 0)), w_spec(0), w_spec(1), w_spec(2)],
        out_specs=(pl.BlockSpec((tm, tn), lambda j, i: (i, j)),
                   pl.BlockSpec((tm, tn), lambda j, i: (i, j))),
        scratch_shapes=[pltpu.VMEM((d, tn), BF16)] * 3,
        compiler_params=_cparams(("arbitrary", "arbitrary")),
        name="conv_in",
    )(a, w_in, w_in, w_in)


LN_ROW_CHUNK = 64
LN_UNROLL = 1
ACC_COL_CHUNK = 1024
MLP_FF_TILE = 512
OUT_PROJ_K_TILE = 1024


def _accumulate(o_ref, lhs, w_ref, gate_ref, hres_ref, alpha, step, n_steps):
    tm, d = o_ref.shape

    @pl.when(step == 0)
    def _():
        o_ref[...] = jnp.zeros_like(o_ref)

    cw = _pick(ACC_COL_CHUNK, d)
    for n in range(d // cw):
        sl = slice(n * cw, (n + 1) * cw)
        o_ref[:, sl] += gate_ref[:, sl] * _dot(lhs, w_ref[:, sl])
    rc = tm // n_steps
    r0 = pl.multiple_of(step * rc, rc)
    o_ref[pl.ds(r0, rc), :] += alpha * hres_ref[...]


def _layer_norm_rows(o_ref, g_ref, b_ref, an_ref, nshift_ref, nscale_ref):
    tm, d = o_ref.shape
    rb = _pick(LN_ROW_CHUNK, tm)

    def body(r, carry):
        r0 = pl.multiple_of(r * rb, rb)
        z = o_ref[pl.ds(r0, rb), :]
        mu = jnp.mean(z, axis=-1, keepdims=True)
        zc = z - mu
        var = jnp.mean(zc * zc, axis=-1, keepdims=True)
        y = zc * lax.rsqrt(var + LN_EPS) * g_ref[...] + b_ref[...]
        o_ref[pl.ds(r0, rb), :] = y
        if an_ref is not None:
            an_ref[pl.ds(r0, rb), :] = (y * (1.0 + nscale_ref[...]) + nshift_ref[...]).astype(BF16)
        return carry

    lax.fori_loop(0, tm // rb, body, 0, unroll=LN_UNROLL)


def _split_tail(refs, has_next):
    if has_next:
        *head, nshift_ref, nscale_ref, o_ref, an_ref = refs
    else:
        *head, o_ref = refs
        nshift_ref = nscale_ref = an_ref = None
    return head, nshift_ref, nscale_ref, o_ref, an_ref


def _mlp_kernel(*refs, alpha, n_steps, has_next):
    head, nshift_ref, nscale_ref, o_ref, an_ref = _split_tail(refs, has_next)
    a_ref, w1_ref, w2_ref, hres_ref, gate_ref, g_ref, b_ref = head
    step = pl.program_id(1)
    hid = jnp.maximum(_dot(a_ref[...], w1_ref[...]), 0.0)
    hid = (hid * hid).astype(BF16)
    _accumulate(o_ref, hid, w2_ref, gate_ref, hres_ref, alpha, step, n_steps)

    @pl.when(step == n_steps - 1)
    def _():
        _layer_norm_rows(o_ref, g_ref, b_ref, an_ref, nshift_ref, nscale_ref)


def _plain_out_kernel(*refs, alpha, n_steps, has_next):
    head, nshift_ref, nscale_ref, o_ref, an_ref = _split_tail(refs, has_next)
    lhs_ref, w_ref, hres_ref, gate_ref, g_ref, b_ref = head
    step = pl.program_id(1)
    _accumulate(o_ref, lhs_ref[...], w_ref, gate_ref, hres_ref, alpha, step, n_steps)

    @pl.when(step == n_steps - 1)
    def _():
        _layer_norm_rows(o_ref, g_ref, b_ref, an_ref, nshift_ref, nscale_ref)


def _conv_out_kernel(*refs, alpha, n_steps, has_next, seq_len):
    head, nshift_ref, nscale_ref, o_ref, an_ref = _split_tail(refs, has_next)
    gb_ref, u_ref, up_ref, un_ref, cw_ref, cb_ref, w_ref, hres_ref, gate_ref, g_ref, b_ref = head
    step = pl.program_id(1)
    tm = u_ref.shape[0]
    u = u_ref[...].astype(F32)
    local = lax.broadcasted_iota(jnp.int32, (tm, 1), 0)
    pos = (local + pl.program_id(0) * tm) % seq_len
    prev_row = up_ref[BF16_SUBLANES - 1:BF16_SUBLANES, :].astype(F32)
    next_row = un_ref[0:1, :].astype(F32)
    u_prev = jnp.where(local == 0, prev_row, pltpu.roll(u, 1, axis=0))
    u_prev = jnp.where(pos == 0, 0.0, u_prev)
    u_next = jnp.where(local == tm - 1, next_row, pltpu.roll(u, tm - 1, axis=0))
    u_next = jnp.where(pos == seq_len - 1, 0.0, u_next)
    conv = u_prev * cw_ref[0:1, :] + u * cw_ref[1:2, :] + u_next * cw_ref[2:3, :] + cb_ref[...]
    lhs = (gb_ref[...].astype(F32) * conv).astype(BF16)
    _accumulate(o_ref, lhs, w_ref, gate_ref, hres_ref, alpha, step, n_steps)

    @pl.when(step == n_steps - 1)
    def _():
        _layer_norm_rows(o_ref, g_ref, b_ref, an_ref, nshift_ref, nscale_ref)


def _residual_ln_call(kernel, name, lead_specs, lead_args, hres, mods, layer, gate_idx, rot_fn,
                      ln_g, ln_b, next_mod, tm, n_steps, alpha, **kernel_kwargs):
    rows, d = hres.shape
    rc = tm // n_steps
    assert rc * n_steps == tm and rc % 8 == 0, (tm, n_steps)
    rot = rot_fn.mod_row(tm)
    in_specs = list(lead_specs) + [
        pl.BlockSpec((rc, d), lambda i, k: (i * n_steps + k, 0)),
        _mod_spec(layer, gate_idx, rot, d),
        _row_spec(d),
        _row_spec(d),
    ]
    args = list(lead_args) + [hres, mods, ln_g, ln_b]
    out_shape = [jax.ShapeDtypeStruct((rows, d), F32)]
    out_specs = [pl.BlockSpec((tm, d), lambda i, k: (i, 0))]
    has_next = next_mod is not None
    if has_next:
        nl, nidx = next_mod
        in_specs += [_mod_spec(nl, nidx, rot, d), _mod_spec(nl, nidx + 1, rot, d)]
        args += [mods, mods]
        out_shape.append(jax.ShapeDtypeStruct((rows, d), BF16))
        out_specs.append(pl.BlockSpec((tm, d), lambda i, k: (i, 0)))
    res = pl.pallas_call(
        functools.partial(kernel, alpha=alpha, n_steps=n_steps, has_next=has_next, **kernel_kwargs),
        out_shape=tuple(out_shape),
        grid=(rows // tm, n_steps),
        in_specs=in_specs,
        out_specs=tuple(out_specs),
        compiler_params=_cparams(("arbitrary", "arbitrary")),
        name=name,
    )(*args)
    return res if has_next else (res[0], None)


def _mlp(a, w1, w2, hres, mods, layer, rot_fn, ln_g, ln_b, next_mod, alpha):
    rows, d = a.shape
    n_ff_tiles, _, tf = w1.shape
    tm = _pick(512, rot_fn.tile_limit)
    lead_specs = [
        pl.BlockSpec((tm, d), lambda i, k: (i, 0)),
        pl.BlockSpec((None, d, tf), lambda i, k: (k, 0, 0)),
        pl.BlockSpec((tf, d), lambda i, k: (k, 0)),
    ]
    return _residual_ln_call(_mlp_kernel, "mlp", lead_specs, [a, w1, w2], hres, mods, layer, 5, rot_fn,
                             ln_g, ln_b, next_mod, tm, n_ff_tiles, alpha)


def _plain_out(lhs, w, hres, mods, layer, rot_fn, ln_g, ln_b, next_mod, alpha):
    rows, kdim = lhs.shape
    d = w.shape[1]
    tm = _pick(512, rot_fn.tile_limit)
    tk = _pick(OUT_PROJ_K_TILE, kdim)
    lead_specs = [
        pl.BlockSpec((tm, tk), lambda i, k: (i, k)),
        pl.BlockSpec((tk, d), lambda i, k: (k, 0)),
    ]
    return _residual_ln_call(_plain_out_kernel, "attn_out", lead_specs, [lhs, w], hres, mods, layer, 2,
                             rot_fn, ln_g, ln_b, next_mod, tm, kdim // tk, alpha)


def _conv_out(gb, u, conv_w, conv_b, w, hres, mods, layer, rot_fn, ln_g, ln_b, next_mod, alpha, seq_len):
    rows, kdim = u.shape
    d = w.shape[1]
    tm = _pick(512, rot_fn.tile_limit)
    tk = _pick(OUT_PROJ_K_TILE, kdim)
    hb = BF16_SUBLANES
    n_halo = rows // hb
    lead_specs = [
        pl.BlockSpec((tm, tk), lambda i, k: (i, k)),
        pl.BlockSpec((tm, tk), lambda i, k: (i, k)),
        pl.BlockSpec((hb, tk), lambda i, k: (jnp.maximum(i * (tm // hb) - 1, 0), k)),
        pl.BlockSpec((hb, tk), lambda i, k: (jnp.minimum((i + 1) * (tm // hb), n_halo - 1), k)),
        pl.BlockSpec((3, tk), lambda i, k: (0, k)),
        pl.BlockSpec((1, tk), lambda i, k: (0, k)),
        pl.BlockSpec((tk, d), lambda i, k: (k, 0)),
    ]
    return _residual_ln_call(_conv_out_kernel, "conv_out", lead_specs, [gb, u, u, u, conv_w, conv_b, w],
                             hres, mods, layer, 2, rot_fn, ln_g, ln_b, next_mod, tm, kdim // tk, alpha,
                             seq_len=seq_len)


def _rms_rope_heads(y, gain_ref, rope_refs, o_ref):
    tm, tn = y.shape
    if rope_refs is not None:
        cos_ref, sin_ref = rope_refs
        lane = lax.broadcasted_iota(jnp.int32, (tm, HEAD_DIM), 1)
        first_half = (lane % (HEAD_DIM // 2)) < (HEAD_DIM // 4)
    for h in range(tn // HEAD_DIM):
        sl = slice(h * HEAD_DIM, (h + 1) * HEAD_DIM)
        x = y[:, sl]
        x = x * lax.rsqrt(jnp.mean(x * x, axis=-1, keepdims=True) + RMS_EPS) * gain_ref[...]
        if rope_refs is not None:
            rot = jnp.where(first_half,
                            pltpu.roll(x, HEAD_DIM - HEAD_DIM // 4, axis=1),
                            pltpu.roll(x, HEAD_DIM // 4, axis=1))
            x = x * cos_ref[...] + rot * sin_ref[...]
        o_ref[:, sl] = x.astype(BF16)


def _qkv_kernel(*refs, n_q_tiles, n_k_tiles, rope):
    if rope:
        a_ref, w_ref, qg_ref, kg_ref, cos_ref, sin_ref, o_ref, ws = refs
        rope_refs = (cos_ref, sin_ref)
    else:
        a_ref, w_ref, qg_ref, kg_ref, o_ref, ws = refs
        rope_refs = None
    j = pl.program_id(0)

    @pl.when(pl.program_id(1) == 0)
    def _():
        ws[...] = w_ref[...].astype(BF16)

    y = _dot(a_ref[...], ws[...])

    if n_q_tiles:
        @pl.when(j < n_q_tiles)
        def _():
            _rms_rope_heads(y, qg_ref, rope_refs, o_ref)

    @pl.when((j >= n_q_tiles) & (j < n_q_tiles + n_k_tiles))
    def _():
        _rms_rope_heads(y, kg_ref, rope_refs, o_ref)

    @pl.when(j >= n_q_tiles + n_k_tiles)
    def _():
        o_ref[...] = y.astype(BF16)


def _qkv(a, w_qkv, q_gain, k_gain, rope_tables, col0, q_dim, kv_dim, seq_len):
    rows, d = a.shape
    n_cols = w_qkv.shape[1] - col0
    tm = _pick(1024, seq_len if rope_tables is not None else rows)
    tn = _pick(512, kv_dim)
    assert col0 % tn == 0 and q_dim % tn == 0
    j0 = col0 // tn
    n_q_tiles = (q_dim - col0) // tn if col0 < q_dim else 0
    n_k_tiles = kv_dim // tn
    rope = rope_tables is not None
    in_specs = [
        pl.BlockSpec((tm, d), lambda j, i: (i, 0)),
        pl.BlockSpec((d, tn), lambda j, i: (0, j + j0)),
        pl.BlockSpec((1, HEAD_DIM), lambda j, i: (0, 0)),
        pl.BlockSpec((1, HEAD_DIM), lambda j, i: (0, 0)),
    ]
    args = [a, w_qkv, q_gain, k_gain]
    if rope:
        tiles_per_seq = seq_len // tm
        tab = pl.BlockSpec((tm, HEAD_DIM), lambda j, i: (i % tiles_per_seq, 0))
        in_specs += [tab, tab]
        args += list(rope_tables)
    return pl.pallas_call(
        functools.partial(_qkv_kernel, n_q_tiles=n_q_tiles, n_k_tiles=n_k_tiles, rope=rope),
        out_shape=jax.ShapeDtypeStruct((rows, n_cols), BF16),
        grid=(n_cols // tn, rows // tm),
        in_specs=in_specs,
        out_specs=pl.BlockSpec((tm, tn), lambda j, i: (i, j)),
        scratch_shapes=[pltpu.VMEM((d, tn), BF16)],
        compiler_params=_cparams(("arbitrary", "arbitrary")),
        name="qkv" if rope else "kv_ctx",
    )(*args)


def _rope_tables(seq_len):
    rows = seq_len // GRID_W
    row = jnp.repeat(jnp.arange(rows), GRID_W)
    col = jnp.tile(jnp.arange(GRID_W), rows)
    half = HEAD_DIM // 2
    inv_freq = ROPE_THETA ** (-jnp.arange(0, half, 2, dtype=F32) / half)

    def axis_angles(pos):
        ang = pos.astype(F32)[:, None] * inv_freq[None, :]
        return jnp.concatenate([ang, ang], axis=-1)

    ang = jnp.concatenate([axis_angles(row), axis_angles(col)], axis=-1)
    lane = jnp.arange(HEAD_DIM)
    sign = jnp.where((lane % half) < (HEAD_DIM // 4), -1.0, 1.0).astype(F32)
    return jnp.cos(ang), jnp.sin(ang) * sign[None, :]


_NT = (((1,), (1,)), ((), ()))
ATTN_KEY_CHUNK = 1024
ATTN_Q_TILE = 256


def _lane_fold(x, op):
    assert x.shape[1] % HEAD_DIM == 0
    acc = x[:, :HEAD_DIM]
    for t in range(1, x.shape[1] // HEAD_DIM):
        acc = op(acc, x[:, t * HEAD_DIM:(t + 1) * HEAD_DIM])
    return acc


def _score_tile(q_ref, kc_ref, kl_ref, s_ref, m_ref, key_chunk):
    ctx_len, seq_len = kc_ref.shape[0], kl_ref.shape[0]
    q = jnp.concatenate([q_ref[:, g * HEAD_DIM:(g + 1) * HEAD_DIM] for g in range(GQA_GROUP)], axis=0)
    segs = [(kc_ref, 0, ctx_len, 0)] + [(kl_ref, r, key_chunk, ctx_len + r) for r in range(0, seq_len, key_chunk)]
    m_lane = None
    for k_ref, r0, cnt, c0 in segs:
        s = lax.dot_general(q, k_ref[r0:r0 + cnt, :], _NT, preferred_element_type=F32)
        s_ref[:, c0:c0 + cnt] = s
        fold = _lane_fold(s, jnp.maximum)
        m_lane = fold if m_lane is None else jnp.maximum(m_lane, fold)
    m_ref[...] = jnp.broadcast_to(jnp.max(m_lane, axis=-1, keepdims=True), m_ref.shape)


def _finish_tile(s_ref, m_ref, vc_ref, vl_ref, o_ref, row0, scale, key_chunk):
    ctx_len, seq_len = vc_ref.shape[0], vl_ref.shape[0]
    rows = s_ref.shape[0]
    tq = rows // GQA_GROUP
    segs = [(vc_ref, 0, ctx_len, 0)] + [(vl_ref, r, key_chunk, ctx_len + r) for r in range(0, seq_len, key_chunk)]
    m = m_ref[...]
    l_lane = jnp.zeros((rows, HEAD_DIM), F32)
    o = jnp.zeros((rows, HEAD_DIM), F32)
    for v_ref, r0, cnt, c0 in segs:
        parts = []
        for t in range(cnt // HEAD_DIM):
            st = s_ref[:, c0 + t * HEAD_DIM:c0 + (t + 1) * HEAD_DIM]
            pt = jnp.exp2((st - m) * (scale * LOG2_E))
            l_lane = l_lane + pt
            parts.append(pt.astype(BF16))
        o = o + _dot(jnp.concatenate(parts, axis=1), v_ref[r0:r0 + cnt, :])
    o = o / jnp.sum(l_lane, axis=-1, keepdims=True)
    for g in range(GQA_GROUP):
        o_ref[row0:row0 + tq, g * HEAD_DIM:(g + 1) * HEAD_DIM] = o[g * tq:(g + 1) * tq, :].astype(BF16)


def _attention_kernel(qa_ref, qb_ref, kca_ref, kla_ref, kcb_ref, klb_ref, vc_ref, vl_ref, o_ref,
                      s0_ref, s1_ref, m0_ref, m1_ref, *, scale, key_chunk):
    tq = qa_ref.shape[0]

    @pl.when(pl.program_id(0) == 0)
    def _():
        s0_ref[...] = jnp.zeros_like(s0_ref)
        m0_ref[...] = jnp.zeros_like(m0_ref)

    _score_tile(qa_ref, kca_ref, kla_ref, s1_ref, m1_ref, key_chunk)
    _finish_tile(s0_ref, m0_ref, vc_ref, vl_ref, o_ref, 0, scale, key_chunk)
    _score_tile(qb_ref, kcb_ref, klb_ref, s0_ref, m0_ref, key_chunk)
    _finish_tile(s1_ref, m1_ref, vc_ref, vl_ref, o_ref, tq, scale, key_chunk)


def _attention(qkv_lat, kv_ctx, bsz, seq_len, ctx_len, q_dim, kv_dim):
    n_kv = kv_dim // HEAD_DIM
    gw = GQA_GROUP * HEAD_DIM
    tq = _pick(ATTN_Q_TILE, seq_len // 2)
    tiles = seq_len // tq
    n_tiles = bsz * n_kv * tiles
    k0 = q_dim // HEAD_DIM
    v0 = k0 + n_kv
    key_chunk = _pick(ATTN_KEY_CHUNK, seq_len)
    assert ctx_len % HEAD_DIM == 0 and key_chunk % HEAD_DIM == 0 and tiles % 2 == 0

    def tile_of(n):
        n = jnp.clip(n, 0, n_tiles - 1)
        return n // (n_kv * tiles), (n // tiles) % n_kv, n % tiles

    def at(offset, f):
        return lambda k: f(*tile_of(2 * k + offset))

    q_blk = lambda b, h, t: (b * tiles + t, h)
    kc_blk = lambda b, h, t: (b, h)
    kl_blk = lambda b, h, t: (b, k0 + h)
    rows = GQA_GROUP * tq
    return pl.pallas_call(
        functools.partial(_attention_kernel, scale=HEAD_DIM ** -0.5, key_chunk=key_chunk),
        out_shape=jax.ShapeDtypeStruct((bsz * seq_len, q_dim), BF16),
        grid=(n_tiles // 2 + 1,),
        in_specs=[
            pl.BlockSpec((tq, gw), at(-1, q_blk)),
            pl.BlockSpec((tq, gw), at(0, q_blk)),
            pl.BlockSpec((ctx_len, HEAD_DIM), at(-1, kc_blk)),
            pl.BlockSpec((seq_len, HEAD_DIM), at(-1, kl_blk)),
            pl.BlockSpec((ctx_len, HEAD_DIM), at(0, kc_blk)),
            pl.BlockSpec((seq_len, HEAD_DIM), at(0, kl_blk)),
            pl.BlockSpec((ctx_len, HEAD_DIM), at(-2, lambda b, h, t: (b, n_kv + h))),
            pl.BlockSpec((seq_len, HEAD_DIM), at(-2, lambda b, h, t: (b, v0 + h))),
        ],
        out_specs=pl.BlockSpec((2 * tq, gw), at(-2, lambda b, h, t: ((b * tiles + t) // 2, h))),
        scratch_shapes=[pltpu.VMEM((rows, ctx_len + seq_len), F32), pltpu.VMEM((rows, ctx_len + seq_len), F32),
                        pltpu.VMEM((rows, HEAD_DIM), F32), pltpu.VMEM((rows, HEAD_DIM), F32)],
        compiler_params=_cparams(("arbitrary",)),
        name="attention",
    )(qkv_lat, qkv_lat, kv_ctx, qkv_lat, kv_ctx, qkv_lat, kv_ctx, qkv_lat)


def kernel(x, c, ctx, c_ctx, ada_w, ada_b, ln_g, ln_b, mlp_w1, mlp_w2, conv_in_w, conv_w, conv_b, conv_out_w,
           attn_qkv_w, attn_q_gain, attn_k_gain, attn_out_w):
    bsz, seq_len, d = x.shape
    ctx_len = ctx.shape[1]
    depth = ada_w.shape[0]
    assert depth == 2 and bsz < COND_ROWS, "one conv-mixer layer followed by one attention layer"
    alpha = (2 * depth) ** 0.25
    kv_dim = (attn_qkv_w.shape[2] - d) // 2
    q_dim = d

    cond = jnp.zeros((COND_ROWS, d), F32).at[:bsz].set(c).at[bsz].set(c_ctx)
    mods = _adaln(cond, ada_w, ada_b).reshape(depth, COND_ROWS, N_MOD, 1, d)

    x_lat = x.reshape(bsz * seq_len, d)
    x_ctx = ctx.reshape(bsz * ctx_len, d)
    lat_rot = _RowMap(seq_len, lambda tm: (lambda i: i // (seq_len // tm)))
    ctx_rot = _RowMap(bsz * ctx_len, lambda tm: (lambda i: bsz))
    row = lambda v: v.reshape(1, -1)

    conv_out_wb = conv_out_w[0].astype(BF16)
    attn_out_wb = attn_out_w[0].astype(BF16)
    tf = _pick(MLP_FF_TILE, mlp_w1.shape[2])
    w1b = [mlp_w1[i].reshape(d, -1, tf).transpose(1, 0, 2).astype(BF16) for i in range(depth)]
    w2b = [mlp_w2[i].astype(BF16) for i in range(depth)]

    def layer0(h, rot_fn, sub_len, final_next_mod):
        a = _modulate(h, mods, 0, rot_fn, _pick(512, rot_fn.tile_limit))
        gb, u = _conv_in(a, conv_in_w[0], rot_fn.tile_limit)
        h1, a1 = _conv_out(gb, u, conv_w[0], row(conv_b[0]), conv_out_wb, h, mods, 0, rot_fn,
                           row(ln_g[0, 0]), row(ln_b[0, 0]), (0, 3), alpha, sub_len)
        return _mlp(a1, w1b[0], w2b[0], h1, mods, 0, rot_fn, row(ln_g[0, 1]), row(ln_b[0, 1]),
                    final_next_mod, alpha)

    h_lat, a_lat = layer0(x_lat, lat_rot, seq_len, (1, 0))
    _, a_ctx = layer0(x_ctx, ctx_rot, ctx_len, (1, 0))

    cos, sin_signed = _rope_tables(seq_len)
    qg, kg = row(attn_q_gain[0]), row(attn_k_gain[0])
    qkv_lat = _qkv(a_lat, attn_qkv_w[0], qg, kg, (cos, sin_signed), 0, q_dim, kv_dim, seq_len)
    kv_ctx = _qkv(a_ctx, attn_qkv_w[0], qg, kg, None, q_dim, q_dim, kv_dim, ctx_len)
    o = _attention(qkv_lat, kv_ctx, bsz, seq_len, ctx_len, q_dim, kv_dim)
    h_lat, a_lat = _plain_out(o, attn_out_wb, h_lat, mods, 1, lat_rot, row(ln_g[1, 0]), row(ln_b[1, 0]),
                              (1, 3), alpha)
    h_lat, _ = _mlp(a_lat, w1b[1], w2b[1], h_lat, mods, 1, lat_rot, row(ln_g[1, 1]), row(ln_b[1, 1]),
                    None, alpha)
    return h_lat.reshape(bsz, seq_len, d)
```

```python
import functools
from typing import Any, Callable, NamedTuple

import jax
import jax.numpy as jnp
from jax import lax
from jax.experimental import pallas as pl
from jax.experimental.pallas import tpu as pltpu

F32 = jnp.float32
BF16 = jnp.bfloat16

HEAD_DIM = 128
GQA_GROUP = 4
GRID_W = 64
N_MOD = 6
ROPE_THETA = 10000.0
LN_EPS = 1e-5
RMS_EPS = 1e-6
LOG2_E = 1.4426950408889634
COND_ROWS = 8
BF16_SUBLANES = 16
VMEM_LIMIT = 60 * 1024 * 1024


class _RowMap(NamedTuple):
    tile_limit: int
    mod_row: Callable


def _cparams(sem):
    return pltpu.CompilerParams(dimension_semantics=sem, vmem_limit_bytes=VMEM_LIMIT)


def _pick(pref, n):
    t = min(pref, n)
    while n % t:
        t //= 2
    return t


def _dot(a, b):
    return jnp.dot(a, b, preferred_element_type=F32)


def _adaln_kernel(cond_ref, w_ref, b_ref, o_ref):
    c = cond_ref[...]
    s = (c * jax.nn.sigmoid(c)).astype(BF16)
    o_ref[...] = _dot(s, w_ref[...].astype(BF16)) + b_ref[...]


def _adaln(cond, ada_w, ada_b):
    depth, d, n = ada_w.shape
    tn = _pick(512, n)
    return pl.pallas_call(
        _adaln_kernel,
        out_shape=jax.ShapeDtypeStruct((depth, COND_ROWS, n), F32),
        grid=(depth, n // tn),
        in_specs=[
            pl.BlockSpec((COND_ROWS, d), lambda l, j: (0, 0)),
            pl.BlockSpec((None, d, tn), lambda l, j: (l, 0, j)),
            pl.BlockSpec((None, 1, tn), lambda l, j: (l, 0, j)),
        ],
        out_specs=pl.BlockSpec((None, COND_ROWS, tn), lambda l, j: (l, 0, j)),
        compiler_params=_cparams(("arbitrary", "arbitrary")),
        name="adaln",
    )(cond, ada_w, ada_b.reshape(depth, 1, n))


def _mod_spec(layer, which, row_of_tile, d):
    return pl.BlockSpec((None, None, None, 1, d),
                        lambda i, *_: (layer, row_of_tile(i), which, 0, 0))


def _row_spec(d):
    return pl.BlockSpec((1, d), lambda *_: (0, 0))


CAST_BLOCK_ELEMS = 2 * 1024 * 1024


def _cast_kernel(x_ref, o_ref):
    o_ref[...] = x_ref[...].astype(BF16)


def _to_bf16(w):
    cols = w.shape[-1]
    w2 = w.reshape(-1, cols)
    rows = w2.shape[0]
    tr = _pick(max(CAST_BLOCK_ELEMS // cols, BF16_SUBLANES), rows)
    out = pl.pallas_call(
        _cast_kernel,
        out_shape=jax.ShapeDtypeStruct((rows, cols), BF16),
        grid=(rows // tr,),
        in_specs=[pl.BlockSpec((tr, cols), lambda i: (i, 0))],
        out_specs=pl.BlockSpec((tr, cols), lambda i: (i, 0)),
        compiler_params=_cparams(("arbitrary",)),
        name="cast_bf16",
    )(w2)
    return out.reshape(w.shape)


def _modulate_kernel(x_ref, shift_ref, scale_ref, o_ref):
    o_ref[...] = (x_ref[...] * (1.0 + scale_ref[...]) + shift_ref[...]).astype(BF16)


def _modulate(x, mods, layer, row_of_tile_fn, tm):
    rows, d = x.shape
    rot = row_of_tile_fn.mod_row(tm)
    return pl.pallas_call(
        _modulate_kernel,
        out_shape=jax.ShapeDtypeStruct((rows, d), BF16),
        grid=(rows // tm,),
        in_specs=[
            pl.BlockSpec((tm, d), lambda i: (i, 0)),
            _mod_spec(layer, 0, rot, d),
            _mod_spec(layer, 1, rot, d),
        ],
        out_specs=pl.BlockSpec((tm, d), lambda i: (i, 0)),
        compiler_params=_cparams(("arbitrary",)),
        name="modulate",
    )(x, mods, mods)


def _conv_in_kernel(a_ref, wb_ref, wc_ref, wv_ref, gb_ref, u_ref, wsb, wsc, wsv):
    @pl.when(pl.program_id(1) == 0)
    def _():
        wsb[...] = wb_ref[...].astype(BF16)
        wsc[...] = wc_ref[...].astype(BF16)
        wsv[...] = wv_ref[...].astype(BF16)

    a = a_ref[...]
    gb_ref[...] = _dot(a, wsb[...]).astype(BF16)
    u_ref[...] = (_dot(a, wsc[...]) * _dot(a, wsv[...])).astype(BF16)


def _conv_in(a, w_in, tile_limit):
    rows, d = a.shape
    tm = _pick(1024, tile_limit)
    tn = _pick(256, d)
    nj = d // tn
    out = jax.ShapeDtypeStruct((rows, d), BF16)
    w_spec = lambda t: pl.BlockSpec((d, tn), lambda j, i: (0, j + t * nj))
    return pl.pallas_call(
        _conv_in_kernel,
        out_shape=(out, out),
        grid=(nj, rows // tm),
        in_specs=[pl.BlockSpec((tm, d), lambda j, i: (i, 0)), w_spec(0), w_spec(1), w_spec(2)],
        out_specs=(pl.BlockSpec((tm, tn), lambda j, i: (i, j)),
                   pl.BlockSpec((tm, tn), lambda j, i: (i, j))),
        scratch_shapes=[pltpu.VMEM((d, tn), BF16)] * 3,
        compiler_params=_cparams(("arbitrary", "arbitrary")),
        name="conv_in",
    )(a, w_in, w_in, w_in)


LN_TILE_ROWS = 1024
LN_ROW_CHUNK = 64
OUT_SLOTS = 2
ACC_COL_CHUNK = 1024
MLP_FF_TILE = 512
ATTN_OUT_K_TILE = 1024
CONV_OUT_K_TILE = 512


class _LnRefs(NamedTuple):
    hres: Any
    gate: Any
    g: Any
    b: Any
    nshift: Any
    nscale: Any
    h_hbm: Any
    an_hbm: Any
    acc: Any
    ybuf: Any
    ysem: Any
    anbuf: Any
    ansem: Any


def _split_refs(refs, has_next):
    if has_next:
        *head, hres, gate, g, b, nshift, nscale, h_hbm, an_hbm, acc, ybuf, ysem, anbuf, ansem = refs
        return head, _LnRefs(hres, gate, g, b, nshift, nscale, h_hbm, an_hbm, acc, ybuf, ysem, anbuf, ansem)
    *head, hres, gate, g, b, h_hbm, acc, ybuf, ysem = refs
    return head, _LnRefs(hres, gate, g, b, None, None, h_hbm, None, acc, ybuf, ysem, None, None)


def _accumulate(ln, lhs, w_ref, alpha, step, n_steps):
    acc = ln.acc
    tm, d = acc.shape

    @pl.when(step == 0)
    def _():
        acc[...] = jnp.zeros_like(acc)

    cw = _pick(ACC_COL_CHUNK, d)
    for n in range(d // cw):
        sl = slice(n * cw, (n + 1) * cw)
        acc[:, sl] += ln.gate[:, sl] * _dot(lhs, w_ref[:, sl])
    rc = tm // n_steps
    r0 = pl.multiple_of(step * rc, rc)
    acc[pl.ds(r0, rc), :] += alpha * ln.hres[...]


def _layer_norm_rows(ln, row0):
    tm, d = ln.acc.shape
    rb = ln.ybuf.shape[1]

    def copies(slot, r0):
        dst = pl.ds(row0 + r0, rb)
        cs = [pltpu.make_async_copy(ln.ybuf.at[slot], ln.h_hbm.at[dst], ln.ysem.at[slot])]
        if ln.an_hbm is not None:
            cs.append(pltpu.make_async_copy(ln.anbuf.at[slot], ln.an_hbm.at[dst], ln.ansem.at[slot]))
        return cs

    def body(turn, carry):
        for slot in range(OUT_SLOTS):
            r0 = pl.multiple_of((turn * OUT_SLOTS + slot) * rb, rb)

            @pl.when(turn > 0)
            def _():
                for c in copies(slot, r0):
                    c.wait()

            z = ln.acc[pl.ds(r0, rb), :]
            mu = jnp.mean(z, axis=-1, keepdims=True)
            zc = z - mu
            var = jnp.mean(zc * zc, axis=-1, keepdims=True)
            y = zc * lax.rsqrt(var + LN_EPS) * ln.g[...] + ln.b[...]
            ln.ybuf[slot] = y
            if ln.an_hbm is not None:
                ln.anbuf[slot] = (y * (1.0 + ln.nscale[...]) + ln.nshift[...]).astype(BF16)
            for c in copies(slot, r0):
                c.start()
        return carry

    lax.fori_loop(0, tm // (OUT_SLOTS * rb), body, 0)
    for slot in range(OUT_SLOTS):
        for c in copies(slot, 0):
            c.wait()


def _finish_rows(ln, step, n_steps):
    row0 = pl.program_id(0) * ln.acc.shape[0]

    @pl.when(step == n_steps - 1)
    def _():
        _layer_norm_rows(ln, row0)


def _mlp_kernel(*refs, alpha, n_steps, has_next):
    (a_ref, w1_ref, w2_ref), ln = _split_refs(refs, has_next)
    step = pl.program_id(1)
    hid = jnp.maximum(_dot(a_ref[...], w1_ref[...]), 0.0)
    hid = (hid * hid).astype(BF16)
    _accumulate(ln, hid, w2_ref, alpha, step, n_steps)
    _finish_rows(ln, step, n_steps)


def _plain_out_kernel(*refs, alpha, n_steps, has_next):
    (lhs_ref, w_ref), ln = _split_refs(refs, has_next)
    step = pl.program_id(1)
    _accumulate(ln, lhs_ref[...], w_ref, alpha, step, n_steps)
    _finish_rows(ln, step, n_steps)


def _conv_out_kernel(*refs, alpha, n_steps, has_next, seq_len):
    (gb_ref, u_ref, up_ref, un_ref, cw_ref, cb_ref, w_ref), ln = _split_refs(refs, has_next)
    step = pl.program_id(1)
    tm = u_ref.shape[0]
    u = u_ref[...].astype(F32)
    local = lax.broadcasted_iota(jnp.int32, (tm, 1), 0)
    pos = (local + pl.program_id(0) * tm) % seq_len
    prev_row = up_ref[BF16_SUBLANES - 1:BF16_SUBLANES, :].astype(F32)
    next_row = un_ref[0:1, :].astype(F32)
    u_prev = jnp.where(local == 0, prev_row, pltpu.roll(u, 1, axis=0))
    u_prev = jnp.where(pos == 0, 0.0, u_prev)
    u_next = jnp.where(local == tm - 1, next_row, pltpu.roll(u, tm - 1, axis=0))
    u_next = jnp.where(pos == seq_len - 1, 0.0, u_next)
    conv = u_prev * cw_ref[0:1, :] + u * cw_ref[1:2, :] + u_next * cw_ref[2:3, :] + cb_ref[...]
    lhs = (gb_ref[...].astype(F32) * conv).astype(BF16)
    _accumulate(ln, lhs, w_ref, alpha, step, n_steps)
    _finish_rows(ln, step, n_steps)


def _residual_ln_call(kernel, name, lead_specs, lead_args, hres, mods, layer, gate_idx, rot_fn,
                      ln_g, ln_b, next_mod, tm, n_steps, alpha, **kernel_kwargs):
    rows, d = hres.shape
    rc = tm // n_steps
    rb = _pick(LN_ROW_CHUNK, tm // OUT_SLOTS)
    assert rc * n_steps == tm and rc % 8 == 0 and tm % (OUT_SLOTS * rb) == 0, (tm, n_steps, rb)
    rot = rot_fn.mod_row(tm)
    in_specs = list(lead_specs) + [
        pl.BlockSpec((rc, d), lambda i, k: (i * n_steps + k, 0)),
        _mod_spec(layer, gate_idx, rot, d),
        _row_spec(d),
        _row_spec(d),
    ]
    args = list(lead_args) + [hres, mods, ln_g, ln_b]
    out_shape = [jax.ShapeDtypeStruct((rows, d), F32)]
    scratch = [pltpu.VMEM((tm, d), F32), pltpu.VMEM((OUT_SLOTS, rb, d), F32), pltpu.SemaphoreType.DMA((OUT_SLOTS,))]
    has_next = next_mod is not None
    if has_next:
        nl, nidx = next_mod
        in_specs += [_mod_spec(nl, nidx, rot, d), _mod_spec(nl, nidx + 1, rot, d)]
        args += [mods, mods]
        out_shape.append(jax.ShapeDtypeStruct((rows, d), BF16))
        scratch += [pltpu.VMEM((OUT_SLOTS, rb, d), BF16), pltpu.SemaphoreType.DMA((OUT_SLOTS,))]
    res = pl.pallas_call(
        functools.partial(kernel, alpha=alpha, n_steps=n_steps, has_next=has_next, **kernel_kwargs),
        out_shape=tuple(out_shape),
        grid=(rows // tm, n_steps),
        in_specs=in_specs,
        out_specs=tuple(pl.BlockSpec(memory_space=pl.ANY) for _ in out_shape),
        scratch_shapes=scratch,
        compiler_params=_cparams(("arbitrary", "arbitrary")),
        name=name,
    )(*args)
    return res if has_next else (res[0], None)


def _mlp(a, w1, w2, hres, mods, layer, rot_fn, ln_g, ln_b, next_mod, alpha):
    rows, d = a.shape
    d_ff = w1.shape[2]
    tm = _pick(LN_TILE_ROWS, rot_fn.tile_limit)
    tf = _pick(MLP_FF_TILE, d_ff)
    lead_specs = [
        pl.BlockSpec((tm, d), lambda i, k: (i, 0), pipeline_mode=pl.Buffered(1)),
        pl.BlockSpec((None, d, tf), lambda i, k: (layer, 0, k)),
        pl.BlockSpec((None, tf, d), lambda i, k: (layer, k, 0)),
    ]
    return _residual_ln_call(_mlp_kernel, "mlp", lead_specs, [a, w1, w2], hres, mods, layer, 5, rot_fn,
                             ln_g, ln_b, next_mod, tm, d_ff // tf, alpha)


def _plain_out(lhs, w, hres, mods, layer, rot_fn, ln_g, ln_b, next_mod, alpha):
    rows, kdim = lhs.shape
    d = w.shape[1]
    tm = _pick(LN_TILE_ROWS, rot_fn.tile_limit)
    tk = _pick(ATTN_OUT_K_TILE, kdim)
    lead_specs = [
        pl.BlockSpec((tm, tk), lambda i, k: (i, k)),
        pl.BlockSpec((tk, d), lambda i, k: (k, 0)),
    ]
    return _residual_ln_call(_plain_out_kernel, "attn_out", lead_specs, [lhs, w], hres, mods, layer, 2,
                             rot_fn, ln_g, ln_b, next_mod, tm, kdim // tk, alpha)


def _conv_out(gb, u, conv_w, conv_b, w, hres, mods, layer, rot_fn, ln_g, ln_b, next_mod, alpha, seq_len):
    rows, kdim = u.shape
    d = w.shape[1]
    tm = _pick(LN_TILE_ROWS, rot_fn.tile_limit)
    tk = _pick(CONV_OUT_K_TILE, kdim)
    hb = BF16_SUBLANES
    n_halo = rows // hb
    lead_specs = [
        pl.BlockSpec((tm, tk), lambda i, k: (i, k)),
        pl.BlockSpec((tm, tk), lambda i, k: (i, k)),
        pl.BlockSpec((hb, tk), lambda i, k: (jnp.maximum(i * (tm // hb) - 1, 0), k)),
        pl.BlockSpec((hb, tk), lambda i, k: (jnp.minimum((i + 1) * (tm // hb), n_halo - 1), k)),
        pl.BlockSpec((3, tk), lambda i, k: (0, k)),
        pl.BlockSpec((1, tk), lambda i, k: (0, k)),
        pl.BlockSpec((tk, d), lambda i, k: (k, 0)),
    ]
    return _residual_ln_call(_conv_out_kernel, "conv_out", lead_specs, [gb, u, u, u, conv_w, conv_b, w],
                             hres, mods, layer, 2, rot_fn, ln_g, ln_b, next_mod, tm, kdim // tk, alpha,
                             seq_len=seq_len)


def _rms_rope_heads(y, gain_ref, rope_refs, o_ref):
    tm, tn = y.shape
    if rope_refs is not None:
        cos_ref, sin_ref = rope_refs
        lane = lax.broadcasted_iota(jnp.int32, (tm, HEAD_DIM), 1)
        first_half = (lane % (HEAD_DIM // 2)) < (HEAD_DIM // 4)
    for h in range(tn // HEAD_DIM):
        sl = slice(h * HEAD_DIM, (h + 1) * HEAD_DIM)
        x = y[:, sl]
        x = x * lax.rsqrt(jnp.mean(x * x, axis=-1, keepdims=True) + RMS_EPS) * gain_ref[...]
        if rope_refs is not None:
            rot = jnp.where(first_half,
                            pltpu.roll(x, HEAD_DIM - HEAD_DIM // 4, axis=1),
                            pltpu.roll(x, HEAD_DIM // 4, axis=1))
            x = x * cos_ref[...] + rot * sin_ref[...]
        o_ref[:, sl] = x.astype(BF16)


def _qkv_kernel(*refs, n_q_tiles, n_k_tiles, rope):
    if rope:
        a_ref, w_ref, qg_ref, kg_ref, cos_ref, sin_ref, o_ref, ws = refs
        rope_refs = (cos_ref, sin_ref)
    else:
        a_ref, w_ref, qg_ref, kg_ref, o_ref, ws = refs
        rope_refs = None
    j = pl.program_id(0)

    @pl.when(pl.program_id(1) == 0)
    def _():
        ws[...] = w_ref[...].astype(BF16)

    y = _dot(a_ref[...], ws[...])

    if n_q_tiles:
        @pl.when(j < n_q_tiles)
        def _():
            _rms_rope_heads(y, qg_ref, rope_refs, o_ref)

    @pl.when((j >= n_q_tiles) & (j < n_q_tiles + n_k_tiles))
    def _():
        _rms_rope_heads(y, kg_ref, rope_refs, o_ref)

    @pl.when(j >= n_q_tiles + n_k_tiles)
    def _():
        o_ref[...] = y.astype(BF16)


def _qkv(a, w_qkv, q_gain, k_gain, rope_tables, col0, q_dim, kv_dim, seq_len):
    rows, d = a.shape
    n_cols = w_qkv.shape[1] - col0
    tm = _pick(1024, seq_len if rope_tables is not None else rows)
    tn = _pick(512, kv_dim)
    assert col0 % tn == 0 and q_dim % tn == 0
    j0 = col0 // tn
    n_q_tiles = (q_dim - col0) // tn if col0 < q_dim else 0
    n_k_tiles = kv_dim // tn
    rope = rope_tables is not None
    in_specs = [
        pl.BlockSpec((tm, d), lambda j, i: (i, 0)),
        pl.BlockSpec((d, tn), lambda j, i: (0, j + j0)),
        pl.BlockSpec((1, HEAD_DIM), lambda j, i: (0, 0)),
        pl.BlockSpec((1, HEAD_DIM), lambda j, i: (0, 0)),
    ]
    args = [a, w_qkv, q_gain, k_gain]
    if rope:
        tiles_per_seq = seq_len // tm
        tab = pl.BlockSpec((tm, HEAD_DIM), lambda j, i: (i % tiles_per_seq, 0))
        in_specs += [tab, tab]
        args += list(rope_tables)
    return pl.pallas_call(
        functools.partial(_qkv_kernel, n_q_tiles=n_q_tiles, n_k_tiles=n_k_tiles, rope=rope),
        out_shape=jax.ShapeDtypeStruct((rows, n_cols), BF16),
        grid=(n_cols // tn, rows // tm),
        in_specs=in_specs,
        out_specs=pl.BlockSpec((tm, tn), lambda j, i: (i, j)),
        scratch_shapes=[pltpu.VMEM((d, tn), BF16)],
        compiler_params=_cparams(("arbitrary", "arbitrary")),
        name="qkv" if rope else "kv_ctx",
    )(*args)


def _rope_tables(seq_len):
    rows = seq_len // GRID_W
    row = jnp.repeat(jnp.arange(rows), GRID_W)
    col = jnp.tile(jnp.arange(GRID_W), rows)
    half = HEAD_DIM // 2
    inv_freq = ROPE_THETA ** (-jnp.arange(0, half, 2, dtype=F32) / half)

    def axis_angles(pos):
        ang = pos.astype(F32)[:, None] * inv_freq[None, :]
        return jnp.concatenate([ang, ang], axis=-1)

    ang = jnp.concatenate([axis_angles(row), axis_angles(col)], axis=-1)
    lane = jnp.arange(HEAD_DIM)
    sign = jnp.where((lane % half) < (HEAD_DIM // 4), -1.0, 1.0).astype(F32)
    return jnp.cos(ang), jnp.sin(ang) * sign[None, :]


_NT = (((1,), (1,)), ((), ()))
ATTN_KEY_CHUNK = 1024
ATTN_Q_TILE = 256


def _lane_fold(x, op):
    assert x.shape[1] % HEAD_DIM == 0
    acc = x[:, :HEAD_DIM]
    for t in range(1, x.shape[1] // HEAD_DIM):
        acc = op(acc, x[:, t * HEAD_DIM:(t + 1) * HEAD_DIM])
    return acc


def _score_tile(q_ref, kc_ref, kl_ref, s_ref, m_ref, key_chunk):
    ctx_len, seq_len = kc_ref.shape[0], kl_ref.shape[0]
    q = jnp.concatenate([q_ref[:, g * HEAD_DIM:(g + 1) * HEAD_DIM] for g in range(GQA_GROUP)], axis=0)
    segs = [(kc_ref, 0, ctx_len, 0)] + [(kl_ref, r, key_chunk, ctx_len + r) for r in range(0, seq_len, key_chunk)]
    m_lane = None
    for k_ref, r0, cnt, c0 in segs:
        s = lax.dot_general(q, k_ref[r0:r0 + cnt, :], _NT, preferred_element_type=F32)
        s_ref[:, c0:c0 + cnt] = s
        fold = _lane_fold(s, jnp.maximum)
        m_lane = fold if m_lane is None else jnp.maximum(m_lane, fold)
    m_ref[...] = jnp.broadcast_to(jnp.max(m_lane, axis=-1, keepdims=True), m_ref.shape)


def _finish_tile(s_ref, m_ref, vc_ref, vl_ref, o_ref, row0, scale, key_chunk):
    ctx_len, seq_len = vc_ref.shape[0], vl_ref.shape[0]
    rows = s_ref.shape[0]
    tq = rows // GQA_GROUP
    segs = [(vc_ref, 0, ctx_len, 0)] + [(vl_ref, r, key_chunk, ctx_len + r) for r in range(0, seq_len, key_chunk)]
    m = m_ref[...]
    l_lane = jnp.zeros((rows, HEAD_DIM), F32)
    o = jnp.zeros((rows, HEAD_DIM), F32)
    for v_ref, r0, cnt, c0 in segs:
        parts = []
        for t in range(cnt // HEAD_DIM):
            st = s_ref[:, c0 + t * HEAD_DIM:c0 + (t + 1) * HEAD_DIM]
            pt = jnp.exp2((st - m) * (scale * LOG2_E))
            l_lane = l_lane + pt
            parts.append(pt.astype(BF16))
        o = o + _dot(jnp.concatenate(parts, axis=1), v_ref[r0:r0 + cnt, :])
    o = o / jnp.sum(l_lane, axis=-1, keepdims=True)
    for g in range(GQA_GROUP):
        o_ref[row0:row0 + tq, g * HEAD_DIM:(g + 1) * HEAD_DIM] = o[g * tq:(g + 1) * tq, :].astype(BF16)


def _attention_kernel(qa_ref, qb_ref, kca_ref, kla_ref, kcb_ref, klb_ref, vc_ref, vl_ref, o_ref,
                      s0_ref, s1_ref, m0_ref, m1_ref, *, scale, key_chunk):
    tq = qa_ref.shape[0]

    @pl.when(pl.program_id(0) == 0)
    def _():
        s0_ref[...] = jnp.zeros_like(s0_ref)
        m0_ref[...] = jnp.zeros_like(m0_ref)

    _score_tile(qa_ref, kca_ref, kla_ref, s1_ref, m1_ref, key_chunk)
    _finish_tile(s0_ref, m0_ref, vc_ref, vl_ref, o_ref, 0, scale, key_chunk)
    _score_tile(qb_ref, kcb_ref, klb_ref, s0_ref, m0_ref, key_chunk)
    _finish_tile(s1_ref, m1_ref, vc_ref, vl_ref, o_ref, tq, scale, key_chunk)


def _attention(qkv_lat, kv_ctx, bsz, seq_len, ctx_len, q_dim, kv_dim):
    n_kv = kv_dim // HEAD_DIM
    gw = GQA_GROUP * HEAD_DIM
    tq = _pick(ATTN_Q_TILE, seq_len // 2)
    tiles = seq_len // tq
    n_tiles = bsz * n_kv * tiles
    k0 = q_dim // HEAD_DIM
    v0 = k0 + n_kv
    key_chunk = _pick(ATTN_KEY_CHUNK, seq_len)
    assert ctx_len % HEAD_DIM == 0 and key_chunk % HEAD_DIM == 0 and tiles % 2 == 0

    def tile_of(n):
        n = jnp.clip(n, 0, n_tiles - 1)
        return n // (n_kv * tiles), (n // tiles) % n_kv, n % tiles

    def at(offset, f):
        return lambda k: f(*tile_of(2 * k + offset))

    q_blk = lambda b, h, t: (b * tiles + t, h)
    kc_blk = lambda b, h, t: (b, h)
    kl_blk = lambda b, h, t: (b, k0 + h)
    rows = GQA_GROUP * tq
    return pl.pallas_call(
        functools.partial(_attention_kernel, scale=HEAD_DIM ** -0.5, key_chunk=key_chunk),
        out_shape=jax.ShapeDtypeStruct((bsz * seq_len, q_dim), BF16),
        grid=(n_tiles // 2 + 1,),
        in_specs=[
            pl.BlockSpec((tq, gw), at(-1, q_blk)),
            pl.BlockSpec((tq, gw), at(0, q_blk)),
            pl.BlockSpec((ctx_len, HEAD_DIM), at(-1, kc_blk)),
            pl.BlockSpec((seq_len, HEAD_DIM), at(-1, kl_blk)),
            pl.BlockSpec((ctx_len, HEAD_DIM), at(0, kc_blk)),
            pl.BlockSpec((seq_len, HEAD_DIM), at(0, kl_blk)),
            pl.BlockSpec((ctx_len, HEAD_DIM), at(-2, lambda b, h, t: (b, n_kv + h))),
            pl.BlockSpec((seq_len, HEAD_DIM), at(-2, lambda b, h, t: (b, v0 + h))),
        ],
        out_specs=pl.BlockSpec((2 * tq, gw), at(-2, lambda b, h, t: ((b * tiles + t) // 2, h))),
        scratch_shapes=[pltpu.VMEM((rows, ctx_len + seq_len), F32), pltpu.VMEM((rows, ctx_len + seq_len), F32),
                        pltpu.VMEM((rows, HEAD_DIM), F32), pltpu.VMEM((rows, HEAD_DIM), F32)],
        compiler_params=_cparams(("arbitrary",)),
        name="attention",
    )(qkv_lat, qkv_lat, kv_ctx, qkv_lat, kv_ctx, qkv_lat, kv_ctx, qkv_lat)


def kernel(x, c, ctx, c_ctx, ada_w, ada_b, ln_g, ln_b, mlp_w1, mlp_w2, conv_in_w, conv_w, conv_b, conv_out_w,
           attn_qkv_w, attn_q_gain, attn_k_gain, attn_out_w):
    bsz, seq_len, d = x.shape
    ctx_len = ctx.shape[1]
    depth = ada_w.shape[0]
    assert depth == 2 and bsz < COND_ROWS, "one conv-mixer layer followed by one attention layer"
    alpha = (2 * depth) ** 0.25
    kv_dim = (attn_qkv_w.shape[2] - d) // 2
    q_dim = d

    cond = jnp.zeros((COND_ROWS, d), F32).at[:bsz].set(c).at[bsz].set(c_ctx)
    mods = _adaln(cond, ada_w, ada_b).reshape(depth, COND_ROWS, N_MOD, 1, d)

    x_lat = x.reshape(bsz * seq_len, d)
    x_ctx = ctx.reshape(bsz * ctx_len, d)
    lat_rot = _RowMap(seq_len, lambda tm: (lambda i: i // (seq_len // tm)))
    ctx_rot = _RowMap(bsz * ctx_len, lambda tm: (lambda i: bsz))
    row = lambda v: v.reshape(1, -1)

    conv_out_wb = _to_bf16(conv_out_w)[0]
    attn_out_wb = _to_bf16(attn_out_w)[0]
    w1b = _to_bf16(mlp_w1)
    w2b = _to_bf16(mlp_w2)

    def layer0(h, rot_fn, sub_len, final_next_mod):
        a = _modulate(h, mods, 0, rot_fn, _pick(512, rot_fn.tile_limit))
        gb, u = _conv_in(a, conv_in_w[0], rot_fn.tile_limit)
        h1, a1 = _conv_out(gb, u, conv_w[0], row(conv_b[0]), conv_out_wb, h, mods, 0, rot_fn,
                           row(ln_g[0, 0]), row(ln_b[0, 0]), (0, 3), alpha, sub_len)
        return _mlp(a1, w1b, w2b, h1, mods, 0, rot_fn, row(ln_g[0, 1]), row(ln_b[0, 1]),
                    final_next_mod, alpha)

    h_lat, a_lat = layer0(x_lat, lat_rot, seq_len, (1, 0))
    _, a_ctx = layer0(x_ctx, ctx_rot, ctx_len, (1, 0))

    cos, sin_signed = _rope_tables(seq_len)
    qg, kg = row(attn_q_gain[0]), row(attn_k_gain[0])
    qkv_lat = _qkv(a_lat, attn_qkv_w[0], qg, kg, (cos, sin_signed), 0, q_dim, kv_dim, seq_len)
    kv_ctx = _qkv(a_ctx, attn_qkv_w[0], qg, kg, None, q_dim, q_dim, kv_dim, ctx_len)
    o = _attention(qkv_lat, kv_ctx, bsz, seq_len, ctx_len, q_dim, kv_dim)
    h_lat, a_lat = _plain_out(o, attn_out_wb, h_lat, mods, 1, lat_rot, row(ln_g[1, 0]), row(ln_b[1, 0]),
                              (1, 3), alpha)
    h_lat, _ = _mlp(a_lat, w1b, w2b, h_lat, mods, 1, lat_rot, row(ln_g[1, 1]), row(ln_b[1, 1]),
                    None, alpha)
    return h_lat.reshape(bsz, seq_len, d)
```

```python
import functools
from typing import Any, Callable, NamedTuple

import jax
import jax.numpy as jnp
from jax import lax
from jax.experimental import pallas as pl
from jax.experimental.pallas import tpu as pltpu

F32 = jnp.float32
BF16 = jnp.bfloat16

HEAD_DIM = 128
GQA_GROUP = 4
GRID_W = 64
N_MOD = 6
ROPE_THETA = 10000.0
LN_EPS = 1e-5
RMS_EPS = 1e-6
LOG2_E = 1.4426950408889634
COND_ROWS = 8
BF16_SUBLANES = 16
LANES = 128
VMEM_LIMIT = 60 * 1024 * 1024


class _RowMap(NamedTuple):
    tile_limit: int
    mod_row: Callable


def _cparams(sem):
    return pltpu.CompilerParams(dimension_semantics=sem, vmem_limit_bytes=VMEM_LIMIT)


def _pick(pref, n):
    t = min(pref, n)
    while n % t:
        t //= 2
    return t


def _dot(a, b):
    return jnp.dot(a, b, preferred_element_type=F32)


def _adaln_kernel(cond_ref, w_ref, b_ref, o_ref):
    c = cond_ref[...]
    s = (c * jax.nn.sigmoid(c)).astype(BF16)
    o_ref[...] = _dot(s, w_ref[...].astype(BF16)) + b_ref[...]


def _adaln(cond, ada_w, ada_b):
    depth, d, n = ada_w.shape
    tn = _pick(512, n)
    return pl.pallas_call(
        _adaln_kernel,
        out_shape=jax.ShapeDtypeStruct((depth, COND_ROWS, n), F32),
        grid=(depth, n // tn),
        in_specs=[
            pl.BlockSpec((COND_ROWS, d), lambda l, j: (0, 0)),
            pl.BlockSpec((None, d, tn), lambda l, j: (l, 0, j)),
            pl.BlockSpec((None, 1, tn), lambda l, j: (l, 0, j)),
        ],
        out_specs=pl.BlockSpec((None, COND_ROWS, tn), lambda l, j: (l, 0, j)),
        compiler_params=_cparams(("arbitrary", "arbitrary")),
        name="adaln",
    )(cond, ada_w, ada_b.reshape(depth, 1, n))


def _mod_spec(layer, which, row_of_tile, d):
    return pl.BlockSpec((None, None, None, 1, d),
                        lambda i, *_: (layer, row_of_tile(i), which, 0, 0))


def _row_spec(d):
    return pl.BlockSpec((1, d), lambda *_: (0, 0))


CAST_BLOCK_ELEMS = 2 * 1024 * 1024


def _cast_kernel(x_ref, o_ref):
    o_ref[...] = x_ref[...].astype(BF16)


def _to_bf16(w):
    cols = w.shape[-1]
    w2 = w.reshape(-1, cols)
    rows = w2.shape[0]
    tr = _pick(max(CAST_BLOCK_ELEMS // cols, BF16_SUBLANES), rows)
    out = pl.pallas_call(
        _cast_kernel,
        out_shape=jax.ShapeDtypeStruct((rows, cols), BF16),
        grid=(rows // tr,),
        in_specs=[pl.BlockSpec((tr, cols), lambda i: (i, 0))],
        out_specs=pl.BlockSpec((tr, cols), lambda i: (i, 0)),
        compiler_params=_cparams(("arbitrary",)),
        name="cast_bf16",
    )(w2)
    return out.reshape(w.shape)


def _modulate_kernel(x_ref, shift_ref, scale_ref, o_ref):
    o_ref[...] = (x_ref[...] * (1.0 + scale_ref[...]) + shift_ref[...]).astype(BF16)


def _modulate(x, mods, layer, row_of_tile_fn, tm):
    rows, d = x.shape
    rot = row_of_tile_fn.mod_row(tm)
    return pl.pallas_call(
        _modulate_kernel,
        out_shape=jax.ShapeDtypeStruct((rows, d), BF16),
        grid=(rows // tm,),
        in_specs=[
            pl.BlockSpec((tm, d), lambda i: (i, 0)),
            _mod_spec(layer, 0, rot, d),
            _mod_spec(layer, 1, rot, d),
        ],
        out_specs=pl.BlockSpec((tm, d), lambda i: (i, 0)),
        compiler_params=_cparams(("arbitrary",)),
        name="modulate",
    )(x, mods, mods)


def _conv_in_kernel(a_ref, wb_ref, wc_ref, wv_ref, gb_ref, u_ref, wsb, wsc, wsv):
    @pl.when(pl.program_id(1) == 0)
    def _():
        wsb[...] = wb_ref[...].astype(BF16)
        wsc[...] = wc_ref[...].astype(BF16)
        wsv[...] = wv_ref[...].astype(BF16)

    a = a_ref[...]
    gb_ref[...] = _dot(a, wsb[...]).astype(BF16)
    u_ref[...] = (_dot(a, wsc[...]) * _dot(a, wsv[...])).astype(BF16)


def _conv_in(a, w_in, tile_limit):
    rows, d = a.shape
    tm = _pick(1024, tile_limit)
    tn = _pick(256, d)
    nj = d // tn
    out = jax.ShapeDtypeStruct((rows, d), BF16)
    w_spec = lambda t: pl.BlockSpec((d, tn), lambda j, i: (0, j + t * nj))
    return pl.pallas_call(
        _conv_in_kernel,
        out_shape=(out, out),
        grid=(nj, rows // tm),
        in_specs=[pl.BlockSpec((tm, d), lambda j, i: (i, 0)), w_spec(0), w_spec(1), w_spec(2)],
        out_specs=(pl.BlockSpec((tm, tn), lambda j, i: (i, j)),
                   pl.BlockSpec((tm, tn), lambda j, i: (i, j))),
        scratch_shapes=[pltpu.VMEM((d, tn), BF16)] * 3,
        compiler_params=_cparams(("arbitrary", "arbitrary")),
        name="conv_in",
    )(a, w_in, w_in, w_in)


LN_TILE_ROWS = 1024
LN_ROW_CHUNK = 64
OUT_SLOTS = 2
ACC_COL_CHUNK = 1024
MLP_FF_TILE = 512
MLP_ROW_SPLIT = 2
ATTN_OUT_K_TILE = 1024
CONV_OUT_K_TILE = 512


class _LnRefs(NamedTuple):
    hres: Any
    gate: Any
    g: Any
    b: Any
    nshift: Any
    nscale: Any
    h_hbm: Any
    an_hbm: Any
    acc: Any
    ybuf: Any
    ysem: Any
    anbuf: Any
    ansem: Any


def _split_refs(refs, has_next):
    if has_next:
        *head, hres, gate, g, b, nshift, nscale, h_hbm, an_hbm, acc, ybuf, ysem, anbuf, ansem = refs
        return head, _LnRefs(hres, gate, g, b, nshift, nscale, h_hbm, an_hbm, acc, ybuf, ysem, anbuf, ansem)
    *head, hres, gate, g, b, h_hbm, acc, ybuf, ysem = refs
    return head, _LnRefs(hres, gate, g, b, None, None, h_hbm, None, acc, ybuf, ysem, None, None)


def _zero_acc_at_first_step(ln, step):
    @pl.when(step == 0)
    def _():
        ln.acc[...] = jnp.zeros_like(ln.acc)


def _accumulate(ln, lhs_parts, w_ref, alpha, step, n_steps):
    acc = ln.acc
    tm, d = acc.shape
    cw = _pick(ACC_COL_CHUNK, d)
    for rows, lhs in lhs_parts:
        for n in range(d // cw):
            sl = slice(n * cw, (n + 1) * cw)
            acc[rows, sl] += ln.gate[:, sl] * _dot(lhs, w_ref[:, sl])
    rc = tm // n_steps
    r0 = pl.multiple_of(step * rc, rc)
    acc[pl.ds(r0, rc), :] += alpha * ln.hres[...]


def _layer_norm_chunk(ln, slot, rows):
    d = ln.acc.shape[1]
    groups = [slice(t * LANES, (t + 1) * LANES) for t in range(d // LANES)]
    total = None
    for g in groups:
        total = ln.acc[rows, g] if total is None else total + ln.acc[rows, g]
    mu = jnp.sum(total, axis=-1, keepdims=True) / d
    total = None
    for g in groups:
        zc = ln.acc[rows, g] - mu
        total = zc * zc if total is None else total + zc * zc
    rstd = lax.rsqrt(jnp.sum(total, axis=-1, keepdims=True) / d + LN_EPS)
    for g in groups:
        y = (ln.acc[rows, g] - mu) * rstd * ln.g[:, g] + ln.b[:, g]
        ln.ybuf[slot, :, g] = y
        if ln.an_hbm is not None:
            ln.anbuf[slot, :, g] = (y * (1.0 + ln.nscale[:, g]) + ln.nshift[:, g]).astype(BF16)


def _layer_norm_rows(ln, row0):
    tm, d = ln.acc.shape
    rb = ln.ybuf.shape[1]

    def copies(slot, r0):
        dst = pl.ds(row0 + r0, rb)
        cs = [pltpu.make_async_copy(ln.ybuf.at[slot], ln.h_hbm.at[dst], ln.ysem.at[slot])]
        if ln.an_hbm is not None:
            cs.append(pltpu.make_async_copy(ln.anbuf.at[slot], ln.an_hbm.at[dst], ln.ansem.at[slot]))
        return cs

    def body(turn, carry):
        for slot in range(OUT_SLOTS):
            r0 = pl.multiple_of((turn * OUT_SLOTS + slot) * rb, rb)

            @pl.when(turn > 0)
            def _():
                for c in copies(slot, r0):
                    c.wait()

            _layer_norm_chunk(ln, slot, pl.ds(r0, rb))
            for c in copies(slot, r0):
                c.start()
        return carry

    lax.fori_loop(0, tm // (OUT_SLOTS * rb), body, 0)
    for slot in range(OUT_SLOTS):
        for c in copies(slot, 0):
            c.wait()


def _finish_rows(ln, step, n_steps):
    row0 = pl.program_id(0) * ln.acc.shape[0]

    @pl.when(step == n_steps - 1)
    def _():
        _layer_norm_rows(ln, row0)


def _mlp_kernel(*refs, alpha, n_steps, has_next):
    (a_ref, w1_ref, w2_ref), ln = _split_refs(refs, has_next)
    step = pl.program_id(1)
    _zero_acc_at_first_step(ln, step)
    rs = a_ref.shape[0] // MLP_ROW_SPLIT
    hid_parts = []
    for r in range(MLP_ROW_SPLIT):
        rows = slice(r * rs, (r + 1) * rs)
        hid = jnp.maximum(_dot(a_ref[rows, :], w1_ref[...]), 0.0)
        hid_parts.append((rows, (hid * hid).astype(BF16)))
    _accumulate(ln, hid_parts, w2_ref, alpha, step, n_steps)
    _finish_rows(ln, step, n_steps)


def _plain_out_kernel(*refs, alpha, n_steps, has_next):
    (lhs_ref, w_ref), ln = _split_refs(refs, has_next)
    step = pl.program_id(1)
    _zero_acc_at_first_step(ln, step)
    _accumulate(ln, [(slice(None), lhs_ref[...])], w_ref, alpha, step, n_steps)
    _finish_rows(ln, step, n_steps)


def _conv_out_kernel(*refs, alpha, n_steps, has_next, seq_len):
    (gb_ref, u_ref, up_ref, un_ref, cw_ref, cb_ref, w_ref), ln = _split_refs(refs, has_next)
    step = pl.program_id(1)
    _zero_acc_at_first_step(ln, step)
    tm = u_ref.shape[0]
    u = u_ref[...].astype(F32)
    local = lax.broadcasted_iota(jnp.int32, (tm, 1), 0)
    pos = (local + pl.program_id(0) * tm) % seq_len
    prev_row = up_ref[BF16_SUBLANES - 1:BF16_SUBLANES, :].astype(F32)
    next_row = un_ref[0:1, :].astype(F32)
    u_prev = jnp.where(local == 0, prev_row, pltpu.roll(u, 1, axis=0))
    u_prev = jnp.where(pos == 0, 0.0, u_prev)
    u_next = jnp.where(local == tm - 1, next_row, pltpu.roll(u, tm - 1, axis=0))
    u_next = jnp.where(pos == seq_len - 1, 0.0, u_next)
    conv = u_prev * cw_ref[0:1, :] + u * cw_ref[1:2, :] + u_next * cw_ref[2:3, :] + cb_ref[...]
    lhs = (gb_ref[...].astype(F32) * conv).astype(BF16)
    _accumulate(ln, [(slice(None), lhs)], w_ref, alpha, step, n_steps)
    _finish_rows(ln, step, n_steps)


def _residual_ln_call(kernel, name, lead_specs, lead_args, hres, mods, layer, gate_idx, rot_fn,
                      ln_g, ln_b, next_mod, tm, n_steps, alpha, **kernel_kwargs):
    rows, d = hres.shape
    rc = tm // n_steps
    rb = _pick(LN_ROW_CHUNK, tm // OUT_SLOTS)
    assert rc * n_steps == tm and rc % 8 == 0 and tm % (OUT_SLOTS * rb) == 0, (tm, n_steps, rb)
    rot = rot_fn.mod_row(tm)
    in_specs = list(lead_specs) + [
        pl.BlockSpec((rc, d), lambda i, k: (i * n_steps + k, 0)),
        _mod_spec(layer, gate_idx, rot, d),
        _row_spec(d),
        _row_spec(d),
    ]
    args = list(lead_args) + [hres, mods, ln_g, ln_b]
    out_shape = [jax.ShapeDtypeStruct((rows, d), F32)]
    scratch = [pltpu.VMEM((tm, d), F32), pltpu.VMEM((OUT_SLOTS, rb, d), F32), pltpu.SemaphoreType.DMA((OUT_SLOTS,))]
    has_next = next_mod is not None
    if has_next:
        nl, nidx = next_mod
        in_specs += [_mod_spec(nl, nidx, rot, d), _mod_spec(nl, nidx + 1, rot, d)]
        args += [mods, mods]
        out_shape.append(jax.ShapeDtypeStruct((rows, d), BF16))
        scratch += [pltpu.VMEM((OUT_SLOTS, rb, d), BF16), pltpu.SemaphoreType.DMA((OUT_SLOTS,))]
    res = pl.pallas_call(
        functools.partial(kernel, alpha=alpha, n_steps=n_steps, has_next=has_next, **kernel_kwargs),
        out_shape=tuple(out_shape),
        grid=(rows // tm, n_steps),
        in_specs=in_specs,
        out_specs=tuple(pl.BlockSpec(memory_space=pl.ANY) for _ in out_shape),
        scratch_shapes=scratch,
        compiler_params=_cparams(("arbitrary", "arbitrary")),
        name=name,
    )(*args)
    return res if has_next else (res[0], None)


def _mlp(a, w1, w2, hres, mods, layer, rot_fn, ln_g, ln_b, next_mod, alpha):
    rows, d = a.shape
    d_ff = w1.shape[2]
    tm = _pick(LN_TILE_ROWS, rot_fn.tile_limit)
    tf = _pick(MLP_FF_TILE, d_ff)
    lead_specs = [
        pl.BlockSpec((tm, d), lambda i, k: (i, 0), pipeline_mode=pl.Buffered(1)),
        pl.BlockSpec((None, d, tf), lambda i, k: (layer, 0, k)),
        pl.BlockSpec((None, tf, d), lambda i, k: (layer, k, 0)),
    ]
    return _residual_ln_call(_mlp_kernel, "mlp", lead_specs, [a, w1, w2], hres, mods, layer, 5, rot_fn,
                             ln_g, ln_b, next_mod, tm, d_ff // tf, alpha)


def _plain_out(lhs, w, hres, mods, layer, rot_fn, ln_g, ln_b, next_mod, alpha):
    rows, kdim = lhs.shape
    d = w.shape[1]
    tm = _pick(LN_TILE_ROWS, rot_fn.tile_limit)
    tk = _pick(ATTN_OUT_K_TILE, kdim)
    lead_specs = [
        pl.BlockSpec((tm, tk), lambda i, k: (i, k)),
        pl.BlockSpec((tk, d), lambda i, k: (k, 0)),
    ]
    return _residual_ln_call(_plain_out_kernel, "attn_out", lead_specs, [lhs, w], hres, mods, layer, 2,
                             rot_fn, ln_g, ln_b, next_mod, tm, kdim // tk, alpha)


def _conv_out(gb, u, conv_w, conv_b, w, hres, mods, layer, rot_fn, ln_g, ln_b, next_mod, alpha, seq_len):
    rows, kdim = u.shape
    d = w.shape[1]
    tm = _pick(LN_TILE_ROWS, rot_fn.tile_limit)
    tk = _pick(CONV_OUT_K_TILE, kdim)
    hb = BF16_SUBLANES
    n_halo = rows // hb
    lead_specs = [
        pl.BlockSpec((tm, tk), lambda i, k: (i, k)),
        pl.BlockSpec((tm, tk), lambda i, k: (i, k)),
        pl.BlockSpec((hb, tk), lambda i, k: (jnp.maximum(i * (tm // hb) - 1, 0), k)),
        pl.BlockSpec((hb, tk), lambda i, k: (jnp.minimum((i + 1) * (tm // hb), n_halo - 1), k)),
        pl.BlockSpec((3, tk), lambda i, k: (0, k)),
        pl.BlockSpec((1, tk), lambda i, k: (0, k)),
        pl.BlockSpec((tk, d), lambda i, k: (k, 0)),
    ]
    return _residual_ln_call(_conv_out_kernel, "conv_out", lead_specs, [gb, u, u, u, conv_w, conv_b, w],
                             hres, mods, layer, 2, rot_fn, ln_g, ln_b, next_mod, tm, kdim // tk, alpha,
                             seq_len=seq_len)


def _rms_rope_heads(y, gain_ref, rope_refs, o_ref):
    tm, tn = y.shape
    if rope_refs is not None:
        cos_ref, sin_ref = rope_refs
        lane = lax.broadcasted_iota(jnp.int32, (tm, HEAD_DIM), 1)
        first_half = (lane % (HEAD_DIM // 2)) < (HEAD_DIM // 4)
    for h in range(tn // HEAD_DIM):
        sl = slice(h * HEAD_DIM, (h + 1) * HEAD_DIM)
        x = y[:, sl]
        x = x * lax.rsqrt(jnp.mean(x * x, axis=-1, keepdims=True) + RMS_EPS) * gain_ref[...]
        if rope_refs is not None:
            rot = jnp.where(first_half,
                            pltpu.roll(x, HEAD_DIM - HEAD_DIM // 4, axis=1),
                            pltpu.roll(x, HEAD_DIM // 4, axis=1))
            x = x * cos_ref[...] + rot * sin_ref[...]
        o_ref[:, sl] = x.astype(BF16)


def _qkv_kernel(*refs, n_q_tiles, n_k_tiles, rope):
    if rope:
        a_ref, w_ref, qg_ref, kg_ref, cos_ref, sin_ref, o_ref, ws = refs
        rope_refs = (cos_ref, sin_ref)
    else:
        a_ref, w_ref, qg_ref, kg_ref, o_ref, ws = refs
        rope_refs = None
    j = pl.program_id(0)

    @pl.when(pl.program_id(1) == 0)
    def _():
        ws[...] = w_ref[...].astype(BF16)

    y = _dot(a_ref[...], ws[...])

    if n_q_tiles:
        @pl.when(j < n_q_tiles)
        def _():
            _rms_rope_heads(y, qg_ref, rope_refs, o_ref)

    @pl.when((j >= n_q_tiles) & (j < n_q_tiles + n_k_tiles))
    def _():
        _rms_rope_heads(y, kg_ref, rope_refs, o_ref)

    @pl.when(j >= n_q_tiles + n_k_tiles)
    def _():
        o_ref[...] = y.astype(BF16)


def _qkv(a, w_qkv, q_gain, k_gain, rope_tables, col0, q_dim, kv_dim, seq_len):
    rows, d = a.shape
    n_cols = w_qkv.shape[1] - col0
    tm = _pick(1024, seq_len if rope_tables is not None else rows)
    tn = _pick(512, kv_dim)
    assert col0 % tn == 0 and q_dim % tn == 0
    j0 = col0 // tn
    n_q_tiles = (q_dim - col0) // tn if col0 < q_dim else 0
    n_k_tiles = kv_dim // tn
    rope = rope_tables is not None
    in_specs = [
        pl.BlockSpec((tm, d), lambda j, i: (i, 0)),
        pl.BlockSpec((d, tn), lambda j, i: (0, j + j0)),
        pl.BlockSpec((1, HEAD_DIM), lambda j, i: (0, 0)),
        pl.BlockSpec((1, HEAD_DIM), lambda j, i: (0, 0)),
    ]
    args = [a, w_qkv, q_gain, k_gain]
    if rope:
        tiles_per_seq = seq_len // tm
        tab = pl.BlockSpec((tm, HEAD_DIM), lambda j, i: (i % tiles_per_seq, 0))
        in_specs += [tab, tab]
        args += list(rope_tables)
    return pl.pallas_call(
        functools.partial(_qkv_kernel, n_q_tiles=n_q_tiles, n_k_tiles=n_k_tiles, rope=rope),
        out_shape=jax.ShapeDtypeStruct((rows, n_cols), BF16),
        grid=(n_cols // tn, rows // tm),
        in_specs=in_specs,
        out_specs=pl.BlockSpec((tm, tn), lambda j, i: (i, j)),
        scratch_shapes=[pltpu.VMEM((d, tn), BF16)],
        compiler_params=_cparams(("arbitrary", "arbitrary")),
        name="qkv" if rope else "kv_ctx",
    )(*args)


def _rope_tables(seq_len):
    rows = seq_len // GRID_W
    row = jnp.repeat(jnp.arange(rows), GRID_W)
    col = jnp.tile(jnp.arange(GRID_W), rows)
    half = HEAD_DIM // 2
    inv_freq = ROPE_THETA ** (-jnp.arange(0, half, 2, dtype=F32) / half)

    def axis_angles(pos):
        ang = pos.astype(F32)[:, None] * inv_freq[None, :]
        return jnp.concatenate([ang, ang], axis=-1)

    ang = jnp.concatenate([axis_angles(row), axis_angles(col)], axis=-1)
    lane = jnp.arange(HEAD_DIM)
    sign = jnp.where((lane % half) < (HEAD_DIM // 4), -1.0, 1.0).astype(F32)
    return jnp.cos(ang), jnp.sin(ang) * sign[None, :]


_NT = (((1,), (1,)), ((), ()))
ATTN_KEY_CHUNK = 1024
ATTN_Q_TILE = 256


def _lane_fold(x, op):
    assert x.shape[1] % HEAD_DIM == 0
    acc = x[:, :HEAD_DIM]
    for t in range(1, x.shape[1] // HEAD_DIM):
        acc = op(acc, x[:, t * HEAD_DIM:(t + 1) * HEAD_DIM])
    return acc


def _score_tile(q_ref, kc_ref, kl_ref, s_ref, m_ref, key_chunk):
    ctx_len, seq_len = kc_ref.shape[0], kl_ref.shape[0]
    q = jnp.concatenate([q_ref[:, g * HEAD_DIM:(g + 1) * HEAD_DIM] for g in range(GQA_GROUP)], axis=0)
    segs = [(kc_ref, 0, ctx_len, 0)] + [(kl_ref, r, key_chunk, ctx_len + r) for r in range(0, seq_len, key_chunk)]
    m_lane = None
    for k_ref, r0, cnt, c0 in segs:
        s = lax.dot_general(q, k_ref[r0:r0 + cnt, :], _NT, preferred_element_type=F32)
        s_ref[:, c0:c0 + cnt] = s
        fold = _lane_fold(s, jnp.maximum)
        m_lane = fold if m_lane is None else jnp.maximum(m_lane, fold)
    m_ref[...] = jnp.broadcast_to(jnp.max(m_lane, axis=-1, keepdims=True), m_ref.shape)


def _finish_tile(s_ref, m_ref, vc_ref, vl_ref, o_ref, row0, scale, key_chunk):
    ctx_len, seq_len = vc_ref.shape[0], vl_ref.shape[0]
    rows = s_ref.shape[0]
    tq = rows // GQA_GROUP
    segs = [(vc_ref, 0, ctx_len, 0)] + [(vl_ref, r, key_chunk, ctx_len + r) for r in range(0, seq_len, key_chunk)]
    m = m_ref[...]
    l_lane = jnp.zeros((rows, HEAD_DIM), F32)
    o = jnp.zeros((rows, HEAD_DIM), F32)
    for v_ref, r0, cnt, c0 in segs:
        parts = []
        for t in range(cnt // HEAD_DIM):
            st = s_ref[:, c0 + t * HEAD_DIM:c0 + (t + 1) * HEAD_DIM]
            pt = jnp.exp2((st - m) * (scale * LOG2_E))
            l_lane = l_lane + pt
            parts.append(pt.astype(BF16))
        o = o + _dot(jnp.concatenate(parts, axis=1), v_ref[r0:r0 + cnt, :])
    o = o / jnp.sum(l_lane, axis=-1, keepdims=True)
    for g in range(GQA_GROUP):
        o_ref[row0:row0 + tq, g * HEAD_DIM:(g + 1) * HEAD_DIM] = o[g * tq:(g + 1) * tq, :].astype(BF16)


def _attention_kernel(q_ref, kc_ref, kl_ref, vc_ref, vl_ref, o_ref, s0_ref, s1_ref, m0_ref, m1_ref,
                      *, scale, key_chunk):
    k = pl.program_id(0)

    @pl.when(k == 0)
    def _():
        s1_ref[...] = jnp.zeros_like(s1_ref)
        m1_ref[...] = jnp.zeros_like(m1_ref)

    def step(s_new, m_new, s_old, m_old):
        _score_tile(q_ref, kc_ref, kl_ref, s_new, m_new, key_chunk)
        _finish_tile(s_old, m_old, vc_ref, vl_ref, o_ref, 0, scale, key_chunk)

    @pl.when(k % 2 == 0)
    def _():
        step(s0_ref, m0_ref, s1_ref, m1_ref)

    @pl.when(k % 2 == 1)
    def _():
        step(s1_ref, m1_ref, s0_ref, m0_ref)


def _attention(qkv_lat, kv_ctx, bsz, seq_len, ctx_len, q_dim, kv_dim):
    n_kv = kv_dim // HEAD_DIM
    gw = GQA_GROUP * HEAD_DIM
    tq = _pick(ATTN_Q_TILE, seq_len)
    tiles = seq_len // tq
    n_tiles = bsz * n_kv * tiles
    k0 = q_dim // HEAD_DIM
    v0 = k0 + n_kv
    key_chunk = _pick(ATTN_KEY_CHUNK, seq_len)
    assert ctx_len % HEAD_DIM == 0 and key_chunk % HEAD_DIM == 0

    def tile_of(n):
        n = jnp.clip(n, 0, n_tiles - 1)
        return n // (n_kv * tiles), (n // tiles) % n_kv, n % tiles

    def at(offset, f):
        return lambda k: f(*tile_of(k + offset))

    rows = GQA_GROUP * tq
    return pl.pallas_call(
        functools.partial(_attention_kernel, scale=HEAD_DIM ** -0.5, key_chunk=key_chunk),
        out_shape=jax.ShapeDtypeStruct((bsz * seq_len, q_dim), BF16),
        grid=(n_tiles + 1,),
        in_specs=[
            pl.BlockSpec((tq, gw), at(0, lambda b, h, t: (b * tiles + t, h))),
            pl.BlockSpec((ctx_len, HEAD_DIM), at(0, lambda b, h, t: (b, h))),
            pl.BlockSpec((seq_len, HEAD_DIM), at(0, lambda b, h, t: (b, k0 + h))),
            pl.BlockSpec((ctx_len, HEAD_DIM), at(-1, lambda b, h, t: (b, n_kv + h))),
            pl.BlockSpec((seq_len, HEAD_DIM), at(-1, lambda b, h, t: (b, v0 + h))),
        ],
        out_specs=pl.BlockSpec((tq, gw), at(-1, lambda b, h, t: (b * tiles + t, h))),
        scratch_shapes=[pltpu.VMEM((rows, ctx_len + seq_len), F32), pltpu.VMEM((rows, ctx_len + seq_len), F32),
                        pltpu.VMEM((rows, HEAD_DIM), F32), pltpu.VMEM((rows, HEAD_DIM), F32)],
        compiler_params=_cparams(("arbitrary",)),
        name="attention",
    )(qkv_lat, kv_ctx, qkv_lat, kv_ctx, qkv_lat)


def kernel(x, c, ctx, c_ctx, ada_w, ada_b, ln_g, ln_b, mlp_w1, mlp_w2, conv_in_w, conv_w, conv_b, conv_out_w,
           attn_qkv_w, attn_q_gain, attn_k_gain, attn_out_w):
    bsz, seq_len, d = x.shape
    ctx_len = ctx.shape[1]
    depth = ada_w.shape[0]
    assert depth == 2 and bsz < COND_ROWS, "one conv-mixer layer followed by one attention layer"
    alpha = (2 * depth) ** 0.25
    kv_dim = (attn_qkv_w.shape[2] - d) // 2
    q_dim = d

    cond = jnp.zeros((COND_ROWS, d), F32).at[:bsz].set(c).at[bsz].set(c_ctx)
    mods = _adaln(cond, ada_w, ada_b).reshape(depth, COND_ROWS, N_MOD, 1, d)

    x_lat = x.reshape(bsz * seq_len, d)
    x_ctx = ctx.reshape(bsz * ctx_len, d)
    lat_rot = _RowMap(seq_len, lambda tm: (lambda i: i // (seq_len // tm)))
    ctx_rot = _RowMap(bsz * ctx_len, lambda tm: (lambda i: bsz))
    row = lambda v: v.reshape(1, -1)

    conv_out_wb = _to_bf16(conv_out_w)[0]
    attn_out_wb = _to_bf16(attn_out_w)[0]
    w1b = _to_bf16(mlp_w1)
    w2b = _to_bf16(mlp_w2)

    def layer0(h, rot_fn, sub_len, final_next_mod):
        a = _modulate(h, mods, 0, rot_fn, _pick(512, rot_fn.tile_limit))
        gb, u = _conv_in(a, conv_in_w[0], rot_fn.tile_limit)
        h1, a1 = _conv_out(gb, u, conv_w[0], row(conv_b[0]), conv_out_wb, h, mods, 0, rot_fn,
                           row(ln_g[0, 0]), row(ln_b[0, 0]), (0, 3), alpha, sub_len)
        return _mlp(a1, w1b, w2b, h1, mods, 0, rot_fn, row(ln_g[0, 1]), row(ln_b[0, 1]),
                    final_next_mod, alpha)

    h_lat, a_lat = layer0(x_lat, lat_rot, seq_len, (1, 0))
    _, a_ctx = layer0(x_ctx, ctx_rot, ctx_len, (1, 0))

    cos, sin_signed = _rope_tables(seq_len)
    qg, kg = row(attn_q_gain[0]), row(attn_k_gain[0])
    qkv_lat = _qkv(a_lat, attn_qkv_w[0], qg, kg, (cos, sin_signed), 0, q_dim, kv_dim, seq_len)
    kv_ctx = _qkv(a_ctx, attn_qkv_w[0], qg, kg, None, q_dim, q_dim, kv_dim, ctx_len)
    o = _attention(qkv_lat, kv_ctx, bsz, seq_len, ctx_len, q_dim, kv_dim)
    h_lat, a_lat = _plain_out(o, attn_out_wb, h_lat, mods, 1, lat_rot, row(ln_g[1, 0]), row(ln_b[1, 0]),
                              (1, 3), alpha)
    h_lat, _ = _mlp(a_lat, w1b, w2b, h_lat, mods, 1, lat_rot, row(ln_g[1, 1]), row(ln_b[1, 1]),
                    None, alpha)
    return h_lat.reshape(bsz, seq_len, d)
```

```python
import functools
from typing import Any, Callable, NamedTuple

import jax
import jax.numpy as jnp
from jax import lax
from jax.experimental import pallas as pl
from jax.experimental.pallas import tpu as pltpu

F32 = jnp.float32
BF16 = jnp.bfloat16

HEAD_DIM = 128
GQA_GROUP = 4
GRID_W = 64
N_MOD = 6
ROPE_THETA = 10000.0
LN_EPS = 1e-5
RMS_EPS = 1e-6
LOG2_E = 1.4426950408889634
COND_ROWS = 8
BF16_SUBLANES = 16
LANES = 128
VMEM_LIMIT = 60 * 1024 * 1024


class _RowMap(NamedTuple):
    tile_limit: int
    mod_row: Callable


def _cparams(sem):
    return pltpu.CompilerParams(dimension_semantics=sem, vmem_limit_bytes=VMEM_LIMIT)


def _pick(pref, n):
    t = min(pref, n)
    while n % t:
        t //= 2
    return t


def _dot(a, b):
    return jnp.dot(a, b, preferred_element_type=F32)


def _adaln_kernel(cond_ref, w_ref, b_ref, o_ref):
    c = cond_ref[...]
    s = (c * jax.nn.sigmoid(c)).astype(BF16)
    o_ref[...] = _dot(s, w_ref[...].astype(BF16)) + b_ref[...]


def _adaln(cond, ada_w, ada_b):
    depth, d, n = ada_w.shape
    tn = _pick(512, n)
    return pl.pallas_call(
        _adaln_kernel,
        out_shape=jax.ShapeDtypeStruct((depth, COND_ROWS, n), F32),
        grid=(depth, n // tn),
        in_specs=[
            pl.BlockSpec((COND_ROWS, d), lambda l, j: (0, 0)),
            pl.BlockSpec((None, d, tn), lambda l, j: (l, 0, j)),
            pl.BlockSpec((None, 1, tn), lambda l, j: (l, 0, j)),
        ],
        out_specs=pl.BlockSpec((None, COND_ROWS, tn), lambda l, j: (l, 0, j)),
        compiler_params=_cparams(("arbitrary", "arbitrary")),
        name="adaln",
    )(cond, ada_w, ada_b.reshape(depth, 1, n))


def _mod_spec(layer, which, row_of_tile, d):
    return pl.BlockSpec((None, None, None, 1, d),
                        lambda i, *_: (layer, row_of_tile(i), which, 0, 0))


def _row_spec(d):
    return pl.BlockSpec((1, d), lambda *_: (0, 0))


CAST_BLOCK_ELEMS = 2 * 1024 * 1024


def _cast_kernel(x_ref, o_ref):
    o_ref[...] = x_ref[...].astype(BF16)


def _to_bf16(w):
    cols = w.shape[-1]
    w2 = w.reshape(-1, cols)
    rows = w2.shape[0]
    want = max(CAST_BLOCK_ELEMS // cols, BF16_SUBLANES)
    tr = _pick(1 << (want.bit_length() - 1), rows)
    out = pl.pallas_call(
        _cast_kernel,
        out_shape=jax.ShapeDtypeStruct((rows, cols), BF16),
        grid=(rows // tr,),
        in_specs=[pl.BlockSpec((tr, cols), lambda i: (i, 0))],
        out_specs=pl.BlockSpec((tr, cols), lambda i: (i, 0)),
        compiler_params=_cparams(("arbitrary",)),
        name="cast_bf16",
    )(w2)
    return out.reshape(w.shape)


def _modulate_kernel(x_ref, shift_ref, scale_ref, o_ref):
    o_ref[...] = (x_ref[...] * (1.0 + scale_ref[...]) + shift_ref[...]).astype(BF16)


def _modulate(x, mods, layer, row_of_tile_fn, tm):
    rows, d = x.shape
    rot = row_of_tile_fn.mod_row(tm)
    return pl.pallas_call(
        _modulate_kernel,
        out_shape=jax.ShapeDtypeStruct((rows, d), BF16),
        grid=(rows // tm,),
        in_specs=[
            pl.BlockSpec((tm, d), lambda i: (i, 0)),
            _mod_spec(layer, 0, rot, d),
            _mod_spec(layer, 1, rot, d),
        ],
        out_specs=pl.BlockSpec((tm, d), lambda i: (i, 0)),
        compiler_params=_cparams(("arbitrary",)),
        name="modulate",
    )(x, mods, mods)


def _conv_in_kernel(*refs, n_jobs):
    a_ref, wb_ref, wc_ref, wv_ref = refs[:4]
    job_in = refs[4:4 + n_jobs]
    gb_ref, u_ref = refs[4 + n_jobs:6 + n_jobs]
    job_out = refs[6 + n_jobs:]
    a = a_ref[...]
    gb_ref[...] = _dot(a, wb_ref[...]).astype(BF16)
    u_ref[...] = (_dot(a, wc_ref[...]) * _dot(a, wv_ref[...])).astype(BF16)
    for x_ref, o_ref in zip(job_in, job_out):
        o_ref[...] = x_ref[...].astype(BF16)


def _conv_in(a, w_in, tile_limit, cast_jobs=()):
    rows, d = a.shape
    tm = _pick(1024, tile_limit)
    tn = _pick(256, d)
    nj, ni = d // tn, rows // tm
    jobs2d = [w.reshape(-1, w.shape[-1]) for w in cast_jobs]
    for w2 in jobs2d:
        assert w2.shape[0] % (nj * ni * BF16_SUBLANES) == 0, (w2.shape, nj, ni)
    job_specs = [pl.BlockSpec((w2.shape[0] // (nj * ni), w2.shape[1]), lambda j, i: (j * ni + i, 0)) for w2 in jobs2d]
    out = jax.ShapeDtypeStruct((rows, d), BF16)
    tile_spec = pl.BlockSpec((tm, tn), lambda j, i: (i, j))
    w_spec = lambda t: pl.BlockSpec((d, tn), lambda j, i: (0, j + t * nj))
    gb, u, *casted = pl.pallas_call(
        functools.partial(_conv_in_kernel, n_jobs=len(jobs2d)),
        out_shape=(out, out, *(jax.ShapeDtypeStruct(w2.shape, BF16) for w2 in jobs2d)),
        grid=(nj, ni),
        in_specs=[pl.BlockSpec((tm, d), lambda j, i: (i, 0)), w_spec(0), w_spec(1), w_spec(2), *job_specs],
        out_specs=(tile_spec, tile_spec, *job_specs),
        compiler_params=_cparams(("arbitrary", "arbitrary")),
        name="conv_in",
    )(a, w_in, w_in, w_in, *jobs2d)
    return gb, u, [c.reshape(w.shape) for c, w in zip(casted, cast_jobs)]


LN_TILE_ROWS = 1024
LN_ROW_CHUNK = 64
OUT_SLOTS = 2
ACC_COL_CHUNK = 1024
MLP_FF_TILE = 512
MLP_ROW_SPLIT = 2
ATTN_OUT_K_TILE = 1024
CONV_OUT_K_TILE = 512


class _LnRefs(NamedTuple):
    hres: Any
    gate: Any
    g: Any
    b: Any
    nshift: Any
    nscale: Any
    h_hbm: Any
    an_hbm: Any
    acc: Any
    ybuf: Any
    ysem: Any
    anbuf: Any
    ansem: Any


def _split_refs(refs, has_next):
    if has_next:
        *head, hres, gate, g, b, nshift, nscale, h_hbm, an_hbm, acc, ybuf, ysem, anbuf, ansem = refs
        return head, _LnRefs(hres, gate, g, b, nshift, nscale, h_hbm, an_hbm, acc, ybuf, ysem, anbuf, ansem)
    *head, hres, gate, g, b, h_hbm, acc, ybuf, ysem = refs
    return head, _LnRefs(hres, gate, g, b, None, None, h_hbm, None, acc, ybuf, ysem, None, None)


def _zero_acc_at_first_step(ln, step):
    @pl.when(step == 0)
    def _():
        ln.acc[...] = jnp.zeros_like(ln.acc)


def _accumulate(ln, lhs_parts, w_ref, alpha, step, n_steps):
    acc = ln.acc
    tm, d = acc.shape
    cw = _pick(ACC_COL_CHUNK, d)
    for rows, lhs in lhs_parts:
        for n in range(d // cw):
            sl = slice(n * cw, (n + 1) * cw)
            acc[rows, sl] += ln.gate[:, sl] * _dot(lhs, w_ref[:, sl])
    rc = tm // n_steps
    r0 = pl.multiple_of(step * rc, rc)
    acc[pl.ds(r0, rc), :] += alpha * ln.hres[...]


def _layer_norm_chunk(ln, slot, rows):
    d = ln.acc.shape[1]
    groups = [slice(t * LANES, (t + 1) * LANES) for t in range(d // LANES)]
    total = None
    for g in groups:
        total = ln.acc[rows, g] if total is None else total + ln.acc[rows, g]
    mu = jnp.sum(total, axis=-1, keepdims=True) / d
    total = None
    for g in groups:
        zc = ln.acc[rows, g] - mu
        total = zc * zc if total is None else total + zc * zc
    rstd = lax.rsqrt(jnp.sum(total, axis=-1, keepdims=True) / d + LN_EPS)
    for g in groups:
        y = (ln.acc[rows, g] - mu) * rstd * ln.g[:, g] + ln.b[:, g]
        ln.ybuf[slot, :, g] = y
        if ln.an_hbm is not None:
            ln.anbuf[slot, :, g] = (y * (1.0 + ln.nscale[:, g]) + ln.nshift[:, g]).astype(BF16)


def _layer_norm_rows(ln, row0):
    tm, d = ln.acc.shape
    rb = ln.ybuf.shape[1]

    def copies(slot, r0):
        dst = pl.ds(row0 + r0, rb)
        cs = [pltpu.make_async_copy(ln.ybuf.at[slot], ln.h_hbm.at[dst], ln.ysem.at[slot])]
        if ln.an_hbm is not None:
            cs.append(pltpu.make_async_copy(ln.anbuf.at[slot], ln.an_hbm.at[dst], ln.ansem.at[slot]))
        return cs

    def body(turn, carry):
        for slot in range(OUT_SLOTS):
            r0 = pl.multiple_of((turn * OUT_SLOTS + slot) * rb, rb)

            @pl.when(turn > 0)
            def _():
                for c in copies(slot, r0):
                    c.wait()

            _layer_norm_chunk(ln, slot, pl.ds(r0, rb))
            for c in copies(slot, r0):
                c.start()
        return carry

    lax.fori_loop(0, tm // (OUT_SLOTS * rb), body, 0)
    for slot in range(OUT_SLOTS):
        for c in copies(slot, 0):
            c.wait()


def _finish_rows(ln, step, n_steps):
    row0 = pl.program_id(0) * ln.acc.shape[0]

    @pl.when(step == n_steps - 1)
    def _():
        _layer_norm_rows(ln, row0)


def _mlp_kernel(*refs, alpha, n_steps, has_next):
    (a_ref, w1_ref, w2_ref), ln = _split_refs(refs, has_next)
    step = pl.program_id(1)
    _zero_acc_at_first_step(ln, step)
    rs = a_ref.shape[0] // MLP_ROW_SPLIT
    hid_parts = []
    for r in range(MLP_ROW_SPLIT):
        rows = slice(r * rs, (r + 1) * rs)
        hid = jnp.maximum(_dot(a_ref[rows, :], w1_ref[...]), 0.0)
        hid_parts.append((rows, (hid * hid).astype(BF16)))
    _accumulate(ln, hid_parts, w2_ref, alpha, step, n_steps)
    _finish_rows(ln, step, n_steps)


def _plain_out_kernel(*refs, alpha, n_steps, has_next):
    (lhs_ref, w_ref), ln = _split_refs(refs, has_next)
    step = pl.program_id(1)
    _zero_acc_at_first_step(ln, step)
    _accumulate(ln, [(slice(None), lhs_ref[...])], w_ref, alpha, step, n_steps)
    _finish_rows(ln, step, n_steps)


def _conv_out_kernel(*refs, alpha, n_steps, has_next, seq_len):
    (gb_ref, u_ref, up_ref, un_ref, cw_ref, cb_ref, w_ref), ln = _split_refs(refs, has_next)
    step = pl.program_id(1)
    _zero_acc_at_first_step(ln, step)
    tm = u_ref.shape[0]
    u = u_ref[...].astype(F32)
    local = lax.broadcasted_iota(jnp.int32, (tm, 1), 0)
    pos = (local + pl.program_id(0) * tm) % seq_len
    prev_row = up_ref[BF16_SUBLANES - 1:BF16_SUBLANES, :].astype(F32)
    next_row = un_ref[0:1, :].astype(F32)
    u_prev = jnp.where(local == 0, prev_row, pltpu.roll(u, 1, axis=0))
    u_prev = jnp.where(pos == 0, 0.0, u_prev)
    u_next = jnp.where(local == tm - 1, next_row, pltpu.roll(u, tm - 1, axis=0))
    u_next = jnp.where(pos == seq_len - 1, 0.0, u_next)
    conv = u_prev * cw_ref[0:1, :] + u * cw_ref[1:2, :] + u_next * cw_ref[2:3, :] + cb_ref[...]
    lhs = (gb_ref[...].astype(F32) * conv).astype(BF16)
    _accumulate(ln, [(slice(None), lhs)], w_ref, alpha, step, n_steps)
    _finish_rows(ln, step, n_steps)


def _residual_ln_call(kernel, name, lead_specs, lead_args, hres, mods, layer, gate_idx, rot_fn,
                      ln_g, ln_b, next_mod, tm, n_steps, alpha, **kernel_kwargs):
    rows, d = hres.shape
    rc = tm // n_steps
    rb = _pick(LN_ROW_CHUNK, tm // OUT_SLOTS)
    assert rc * n_steps == tm and rc % 8 == 0 and tm % (OUT_SLOTS * rb) == 0, (tm, n_steps, rb)
    rot = rot_fn.mod_row(tm)
    in_specs = list(lead_specs) + [
        pl.BlockSpec((rc, d), lambda i, k: (i * n_steps + k, 0)),
        _mod_spec(layer, gate_idx, rot, d),
        _row_spec(d),
        _row_spec(d),
    ]
    args = list(lead_args) + [hres, mods, ln_g, ln_b]
    out_shape = [jax.ShapeDtypeStruct((rows, d), F32)]
    scratch = [pltpu.VMEM((tm, d), F32), pltpu.VMEM((OUT_SLOTS, rb, d), F32), pltpu.SemaphoreType.DMA((OUT_SLOTS,))]
    has_next = next_mod is not None
    if has_next:
        nl, nidx = next_mod
        in_specs += [_mod_spec(nl, nidx, rot, d), _mod_spec(nl, nidx + 1, rot, d)]
        args += [mods, mods]
        out_shape.append(jax.ShapeDtypeStruct((rows, d), BF16))
        scratch += [pltpu.VMEM((OUT_SLOTS, rb, d), BF16), pltpu.SemaphoreType.DMA((OUT_SLOTS,))]
    res = pl.pallas_call(
        functools.partial(kernel, alpha=alpha, n_steps=n_steps, has_next=has_next, **kernel_kwargs),
        out_shape=tuple(out_shape),
        grid=(rows // tm, n_steps),
        in_specs=in_specs,
        out_specs=tuple(pl.BlockSpec(memory_space=pl.ANY) for _ in out_shape),
        scratch_shapes=scratch,
        compiler_params=_cparams(("arbitrary", "arbitrary")),
        name=name,
    )(*args)
    return res if has_next else (res[0], None)


def _mlp(a, w1, w2, hres, mods, layer, rot_fn, ln_g, ln_b, next_mod, alpha):
    rows, d = a.shape
    d_ff = w1.shape[2]
    tm = _pick(LN_TILE_ROWS, rot_fn.tile_limit)
    tf = _pick(MLP_FF_TILE, d_ff)
    lead_specs = [
        pl.BlockSpec((tm, d), lambda i, k: (i, 0), pipeline_mode=pl.Buffered(1)),
        pl.BlockSpec((None, d, tf), lambda i, k: (layer, 0, k)),
        pl.BlockSpec((None, tf, d), lambda i, k: (layer, k, 0)),
    ]
    return _residual_ln_call(_mlp_kernel, "mlp", lead_specs, [a, w1, w2], hres, mods, layer, 5, rot_fn,
                             ln_g, ln_b, next_mod, tm, d_ff // tf, alpha)


def _plain_out(lhs, w, hres, mods, layer, rot_fn, ln_g, ln_b, next_mod, alpha):
    rows, kdim = lhs.shape
    d = w.shape[1]
    tm = _pick(LN_TILE_ROWS, rot_fn.tile_limit)
    tk = _pick(ATTN_OUT_K_TILE, kdim)
    lead_specs = [
        pl.BlockSpec((tm, tk), lambda i, k: (i, k)),
        pl.BlockSpec((tk, d), lambda i, k: (k, 0)),
    ]
    return _residual_ln_call(_plain_out_kernel, "attn_out", lead_specs, [lhs, w], hres, mods, layer, 2,
                             rot_fn, ln_g, ln_b, next_mod, tm, kdim // tk, alpha)


def _conv_out(gb, u, conv_w, conv_b, w, hres, mods, layer, rot_fn, ln_g, ln_b, next_mod, alpha, seq_len):
    rows, kdim = u.shape
    d = w.shape[1]
    tm = _pick(LN_TILE_ROWS, rot_fn.tile_limit)
    tk = _pick(CONV_OUT_K_TILE, kdim)
    hb = BF16_SUBLANES
    n_halo = rows // hb
    lead_specs = [
        pl.BlockSpec((tm, tk), lambda i, k: (i, k)),
        pl.BlockSpec((tm, tk), lambda i, k: (i, k)),
        pl.BlockSpec((hb, tk), lambda i, k: (jnp.maximum(i * (tm // hb) - 1, 0), k)),
        pl.BlockSpec((hb, tk), lambda i, k: (jnp.minimum((i + 1) * (tm // hb), n_halo - 1), k)),
        pl.BlockSpec((3, tk), lambda i, k: (0, k)),
        pl.BlockSpec((1, tk), lambda i, k: (0, k)),
        pl.BlockSpec((tk, d), lambda i, k: (k, 0)),
    ]
    return _residual_ln_call(_conv_out_kernel, "conv_out", lead_specs, [gb, u, u, u, conv_w, conv_b, w],
                             hres, mods, layer, 2, rot_fn, ln_g, ln_b, next_mod, tm, kdim // tk, alpha,
                             seq_len=seq_len)


def _rms_rope_heads(y, gain_ref, rope_refs, o_ref):
    tm, tn = y.shape
    if rope_refs is not None:
        cos_ref, sin_ref = rope_refs
        lane = lax.broadcasted_iota(jnp.int32, (tm, HEAD_DIM), 1)
        first_half = (lane % (HEAD_DIM // 2)) < (HEAD_DIM // 4)
    for h in range(tn // HEAD_DIM):
        sl = slice(h * HEAD_DIM, (h + 1) * HEAD_DIM)
        x = y[:, sl]
        x = x * lax.rsqrt(jnp.mean(x * x, axis=-1, keepdims=True) + RMS_EPS) * gain_ref[...]
        if rope_refs is not None:
            rot = jnp.where(first_half,
                            pltpu.roll(x, HEAD_DIM - HEAD_DIM // 4, axis=1),
                            pltpu.roll(x, HEAD_DIM // 4, axis=1))
            x = x * cos_ref[...] + rot * sin_ref[...]
        o_ref[:, sl] = x.astype(BF16)


def _qkv_kernel(*refs, n_q_tiles, n_k_tiles, rope):
    if rope:
        a_ref, w_ref, qg_ref, kg_ref, cos_ref, sin_ref, o_ref, ws, y0_ref, y1_ref = refs
        rope_refs = (cos_ref, sin_ref)
    else:
        a_ref, w_ref, qg_ref, kg_ref, o_ref, ws, y0_ref, y1_ref = refs
        rope_refs = None
    j, i = pl.program_id(0), pl.program_id(1)

    @pl.when(i == 0)
    def _():
        ws[...] = w_ref[...].astype(BF16)

    @pl.when((i == 0) & (j == 0))
    def _():
        y1_ref[...] = jnp.zeros_like(y1_ref)

    def plain(y):
        o_ref[...] = y.astype(BF16)

    kinds = [((j >= n_q_tiles) & (j < n_q_tiles + n_k_tiles), lambda y: _rms_rope_heads(y, kg_ref, rope_refs, o_ref)),
             (j >= n_q_tiles + n_k_tiles, plain)]
    if n_q_tiles:
        kinds.append((j < n_q_tiles, lambda y: _rms_rope_heads(y, qg_ref, rope_refs, o_ref)))
    for parity, (y_new, y_old) in enumerate([(y0_ref, y1_ref), (y1_ref, y0_ref)]):
        for is_kind, epilogue in kinds:
            @pl.when((i % 2 == parity) & is_kind)
            def _(y_new=y_new, y_old=y_old, epilogue=epilogue):
                y_new[...] = _dot(a_ref[...], ws[...])
                epilogue(y_old[...])


def _qkv(a, w_qkv, q_gain, k_gain, rope_tables, col0, q_dim, kv_dim, seq_len):
    rows, d = a.shape
    n_cols = w_qkv.shape[1] - col0
    tm = _pick(1024, seq_len if rope_tables is not None else rows)
    tn = _pick(512, kv_dim)
    assert col0 % tn == 0 and q_dim % tn == 0
    j0 = col0 // tn
    n_q_tiles = (q_dim - col0) // tn if col0 < q_dim else 0
    n_k_tiles = kv_dim // tn
    rope = rope_tables is not None
    n_row_tiles = rows // tm
    in_specs = [
        pl.BlockSpec((tm, d), lambda j, i: (jnp.minimum(i, n_row_tiles - 1), 0)),
        pl.BlockSpec((d, tn), lambda j, i: (0, j + j0)),
        pl.BlockSpec((1, HEAD_DIM), lambda j, i: (0, 0)),
        pl.BlockSpec((1, HEAD_DIM), lambda j, i: (0, 0)),
    ]
    args = [a, w_qkv, q_gain, k_gain]
    if rope:
        tiles_per_seq = seq_len // tm
        tab = pl.BlockSpec((tm, HEAD_DIM), lambda j, i: (jnp.maximum(i - 1, 0) % tiles_per_seq, 0))
        in_specs += [tab, tab]
        args += list(rope_tables)
    return pl.pallas_call(
        functools.partial(_qkv_kernel, n_q_tiles=n_q_tiles, n_k_tiles=n_k_tiles, rope=rope),
        out_shape=jax.ShapeDtypeStruct((rows, n_cols), BF16),
        grid=(n_cols // tn, n_row_tiles + 1),
        in_specs=in_specs,
        out_specs=pl.BlockSpec((tm, tn), lambda j, i: (jnp.maximum(i - 1, 0), j)),
        scratch_shapes=[pltpu.VMEM((d, tn), BF16), pltpu.VMEM((tm, tn), F32), pltpu.VMEM((tm, tn), F32)],
        compiler_params=_cparams(("arbitrary", "arbitrary")),
        name="qkv" if rope else "kv_ctx",
    )(*args)


def _rope_tables(seq_len):
    rows = seq_len // GRID_W
    row = jnp.repeat(jnp.arange(rows), GRID_W)
    col = jnp.tile(jnp.arange(GRID_W), rows)
    half = HEAD_DIM // 2
    inv_freq = ROPE_THETA ** (-jnp.arange(0, half, 2, dtype=F32) / half)

    def axis_angles(pos):
        ang = pos.astype(F32)[:, None] * inv_freq[None, :]
        return jnp.concatenate([ang, ang], axis=-1)

    ang = jnp.concatenate([axis_angles(row), axis_angles(col)], axis=-1)
    lane = jnp.arange(HEAD_DIM)
    sign = jnp.where((lane % half) < (HEAD_DIM // 4), -1.0, 1.0).astype(F32)
    return jnp.cos(ang), jnp.sin(ang) * sign[None, :]


_NT = (((1,), (1,)), ((), ()))
ATTN_KEY_CHUNK = 1024
ATTN_Q_TILE = 256


def _lane_fold(x, op):
    assert x.shape[1] % HEAD_DIM == 0
    acc = x[:, :HEAD_DIM]
    for t in range(1, x.shape[1] // HEAD_DIM):
        acc = op(acc, x[:, t * HEAD_DIM:(t + 1) * HEAD_DIM])
    return acc


def _score_tile(q_ref, kc_ref, kl_ref, s_ref, m_ref, key_chunk):
    ctx_len, seq_len = kc_ref.shape[0], kl_ref.shape[0]
    q = jnp.concatenate([q_ref[:, g * HEAD_DIM:(g + 1) * HEAD_DIM] for g in range(GQA_GROUP)], axis=0)
    segs = [(kc_ref, 0, ctx_len, 0)] + [(kl_ref, r, key_chunk, ctx_len + r) for r in range(0, seq_len, key_chunk)]
    m_lane = None
    for k_ref, r0, cnt, c0 in segs:
        s = lax.dot_general(q, k_ref[r0:r0 + cnt, :], _NT, preferred_element_type=F32)
        s_ref[:, c0:c0 + cnt] = s
        fold = _lane_fold(s, jnp.maximum)
        m_lane = fold if m_lane is None else jnp.maximum(m_lane, fold)
    m_ref[...] = jnp.broadcast_to(jnp.max(m_lane, axis=-1, keepdims=True), m_ref.shape)


def _finish_tile(s_ref, m_ref, vc_ref, vl_ref, o_ref, row0, scale, key_chunk):
    ctx_len, seq_len = vc_ref.shape[0], vl_ref.shape[0]
    rows = s_ref.shape[0]
    tq = rows // GQA_GROUP
    segs = [(vc_ref, 0, ctx_len, 0)] + [(vl_ref, r, key_chunk, ctx_len + r) for r in range(0, seq_len, key_chunk)]
    m = m_ref[...]
    l_lane = jnp.zeros((rows, HEAD_DIM), F32)
    o = jnp.zeros((rows, HEAD_DIM), F32)
    for v_ref, r0, cnt, c0 in segs:
        parts = []
        for t in range(cnt // HEAD_DIM):
            st = s_ref[:, c0 + t * HEAD_DIM:c0 + (t + 1) * HEAD_DIM]
            pt = jnp.exp2((st - m) * (scale * LOG2_E))
            l_lane = l_lane + pt
            parts.append(pt.astype(BF16))
        o = o + _dot(jnp.concatenate(parts, axis=1), v_ref[r0:r0 + cnt, :])
    o = o / jnp.sum(l_lane, axis=-1, keepdims=True)
    for g in range(GQA_GROUP):
        o_ref[row0:row0 + tq, g * HEAD_DIM:(g + 1) * HEAD_DIM] = o[g * tq:(g + 1) * tq, :].astype(BF16)


def _attention_kernel(q_ref, kc_ref, kl_ref, vc_ref, vl_ref, o_ref, s0_ref, s1_ref, m0_ref, m1_ref,
                      *, scale, key_chunk):
    k = pl.program_id(0)

    @pl.when(k == 0)
    def _():
        s1_ref[...] = jnp.zeros_like(s1_ref)
        m1_ref[...] = jnp.zeros_like(m1_ref)

    def step(s_new, m_new, s_old, m_old):
        _score_tile(q_ref, kc_ref, kl_ref, s_new, m_new, key_chunk)
        _finish_tile(s_old, m_old, vc_ref, vl_ref, o_ref, 0, scale, key_chunk)

    @pl.when(k % 2 == 0)
    def _():
        step(s0_ref, m0_ref, s1_ref, m1_ref)

    @pl.when(k % 2 == 1)
    def _():
        step(s1_ref, m1_ref, s0_ref, m0_ref)


def _attention(qkv_lat, kv_ctx, bsz, seq_len, ctx_len, q_dim, kv_dim):
    n_kv = kv_dim // HEAD_DIM
    gw = GQA_GROUP * HEAD_DIM
    tq = _pick(ATTN_Q_TILE, seq_len)
    tiles = seq_len // tq
    n_tiles = bsz * n_kv * tiles
    k0 = q_dim // HEAD_DIM
    v0 = k0 + n_kv
    key_chunk = _pick(ATTN_KEY_CHUNK, seq_len)
    assert ctx_len % HEAD_DIM == 0 and key_chunk % HEAD_DIM == 0

    def tile_of(n):
        n = jnp.clip(n, 0, n_tiles - 1)
        return n // (n_kv * tiles), (n // tiles) % n_kv, n % tiles

    def at(offset, f):
        return lambda k: f(*tile_of(k + offset))

    rows = GQA_GROUP * tq
    return pl.pallas_call(
        functools.partial(_attention_kernel, scale=HEAD_DIM ** -0.5, key_chunk=key_chunk),
        out_shape=jax.ShapeDtypeStruct((bsz * seq_len, q_dim), BF16),
        grid=(n_tiles + 1,),
        in_specs=[
            pl.BlockSpec((tq, gw), at(0, lambda b, h, t: (b * tiles + t, h))),
            pl.BlockSpec((ctx_len, HEAD_DIM), at(0, lambda b, h, t: (b, h))),
            pl.BlockSpec((seq_len, HEAD_DIM), at(0, lambda b, h, t: (b, k0 + h))),
            pl.BlockSpec((ctx_len, HEAD_DIM), at(-1, lambda b, h, t: (b, n_kv + h))),
            pl.BlockSpec((seq_len, HEAD_DIM), at(-1, lambda b, h, t: (b, v0 + h))),
        ],
        out_specs=pl.BlockSpec((tq, gw), at(-1, lambda b, h, t: (b * tiles + t, h))),
        scratch_shapes=[pltpu.VMEM((rows, ctx_len + seq_len), F32), pltpu.VMEM((rows, ctx_len + seq_len), F32),
                        pltpu.VMEM((rows, HEAD_DIM), F32), pltpu.VMEM((rows, HEAD_DIM), F32)],
        compiler_params=_cparams(("arbitrary",)),
        name="attention",
    )(qkv_lat, kv_ctx, qkv_lat, kv_ctx, qkv_lat)


def kernel(x, c, ctx, c_ctx, ada_w, ada_b, ln_g, ln_b, mlp_w1, mlp_w2, conv_in_w, conv_w, conv_b, conv_out_w,
           attn_qkv_w, attn_q_gain, attn_k_gain, attn_out_w):
    bsz, seq_len, d = x.shape
    ctx_len = ctx.shape[1]
    depth = ada_w.shape[0]
    assert depth == 2 and bsz < COND_ROWS, "one conv-mixer layer followed by one attention layer"
    alpha = (2 * depth) ** 0.25
    kv_dim = (attn_qkv_w.shape[2] - d) // 2
    q_dim = d

    cond = jnp.zeros((COND_ROWS, d), F32).at[:bsz].set(c).at[bsz].set(c_ctx)
    mods = _adaln(cond, ada_w, ada_b).reshape(depth, COND_ROWS, N_MOD, 1, d)

    x_lat = x.reshape(bsz * seq_len, d)
    x_ctx = ctx.reshape(bsz * ctx_len, d)
    lat_rot = _RowMap(seq_len, lambda tm: (lambda i: i // (seq_len // tm)))
    ctx_rot = _RowMap(bsz * ctx_len, lambda tm: (lambda i: bsz))
    row = lambda v: v.reshape(1, -1)

    conv_in_wb = _to_bf16(conv_in_w)[0]

    def layer0(h, rot_fn, sub_len, final_next_mod, weights_bf16):
        a = _modulate(h, mods, 0, rot_fn, _pick(512, rot_fn.tile_limit))
        if weights_bf16 is None:
            gb, u, weights_bf16 = _conv_in(a, conv_in_wb, rot_fn.tile_limit,
                                           (mlp_w1, mlp_w2, conv_out_w, attn_out_w))
        else:
            gb, u, _ = _conv_in(a, conv_in_wb, rot_fn.tile_limit)
        w1b, w2b, conv_out_wb, _ = weights_bf16
        h1, a1 = _conv_out(gb, u, conv_w[0], row(conv_b[0]), conv_out_wb[0], h, mods, 0, rot_fn,
                           row(ln_g[0, 0]), row(ln_b[0, 0]), (0, 3), alpha, sub_len)
        h2, a2 = _mlp(a1, w1b, w2b, h1, mods, 0, rot_fn, row(ln_g[0, 1]), row(ln_b[0, 1]),
                      final_next_mod, alpha)
        return h2, a2, weights_bf16

    h_lat, a_lat, weights_bf16 = layer0(x_lat, lat_rot, seq_len, (1, 0), None)
    _, a_ctx, _ = layer0(x_ctx, ctx_rot, ctx_len, (1, 0), weights_bf16)
    w1b, w2b, _, attn_out_wb = weights_bf16

    cos, sin_signed = _rope_tables(seq_len)
    qg, kg = row(attn_q_gain[0]), row(attn_k_gain[0])
    qkv_lat = _qkv(a_lat, attn_qkv_w[0], qg, kg, (cos, sin_signed), 0, q_dim, kv_dim, seq_len)
    kv_ctx = _qkv(a_ctx, attn_qkv_w[0], qg, kg, None, q_dim, q_dim, kv_dim, ctx_len)
    o = _attention(qkv_lat, kv_ctx, bsz, seq_len, ctx_len, q_dim, kv_dim)
    h_lat, a_lat = _plain_out(o, attn_out_wb[0], h_lat, mods, 1, lat_rot, row(ln_g[1, 0]), row(ln_b[1, 0]),
                              (1, 3), alpha)
    h_lat, _ = _mlp(a_lat, w1b, w2b, h_lat, mods, 1, lat_rot, row(ln_g[1, 1]), row(ln_b[1, 1]),
                    None, alpha)
    return h_lat.reshape(bsz, seq_len, d)
```

```python
import functools
from typing import Any, Callable, NamedTuple

import jax
import jax.numpy as jnp
from jax import lax
from jax.experimental import pallas as pl
from jax.experimental.pallas import tpu as pltpu

F32 = jnp.float32
BF16 = jnp.bfloat16

HEAD_DIM = 128
GQA_GROUP = 4
GRID_W = 64
N_MOD = 6
ROPE_THETA = 10000.0
LN_EPS = 1e-5
RMS_EPS = 1e-6
LOG2_E = 1.4426950408889634
COND_ROWS = 8
BF16_SUBLANES = 16
LANES = 128
VMEM_LIMIT = 60 * 1024 * 1024


class _RowMap(NamedTuple):
    tile_limit: int
    mod_row: Callable


def _cparams(sem):
    return pltpu.CompilerParams(dimension_semantics=sem, vmem_limit_bytes=VMEM_LIMIT)


def _pick(pref, n):
    t = min(pref, n)
    while n % t:
        t //= 2
    return t


def _dot(a, b):
    return jnp.dot(a, b, preferred_element_type=F32)


def _adaln_kernel(cond_ref, w_ref, b_ref, o_ref):
    c = cond_ref[...]
    s = (c * jax.nn.sigmoid(c)).astype(BF16)
    o_ref[...] = _dot(s, w_ref[...].astype(BF16)) + b_ref[...]


def _adaln(cond, ada_w, ada_b):
    depth, d, n = ada_w.shape
    tn = _pick(512, n)
    return pl.pallas_call(
        _adaln_kernel,
        out_shape=jax.ShapeDtypeStruct((depth, COND_ROWS, n), F32),
        grid=(depth, n // tn),
        in_specs=[
            pl.BlockSpec((COND_ROWS, d), lambda l, j: (0, 0)),
            pl.BlockSpec((None, d, tn), lambda l, j: (l, 0, j)),
            pl.BlockSpec((None, 1, tn), lambda l, j: (l, 0, j)),
        ],
        out_specs=pl.BlockSpec((None, COND_ROWS, tn), lambda l, j: (l, 0, j)),
        compiler_params=_cparams(("arbitrary", "arbitrary")),
        name="adaln",
    )(cond, ada_w, ada_b.reshape(depth, 1, n))


def _mod_spec(layer, which, row_of_tile, d):
    return pl.BlockSpec((None, None, None, 1, d),
                        lambda i, *_: (layer, row_of_tile(i), which, 0, 0))


def _row_spec(d):
    return pl.BlockSpec((1, d), lambda *_: (0, 0))


CAST_BLOCK_ELEMS = 2 * 1024 * 1024


def _cast_kernel(x_ref, o_ref):
    o_ref[...] = x_ref[...].astype(BF16)


def _to_bf16(w):
    cols = w.shape[-1]
    w2 = w.reshape(-1, cols)
    rows = w2.shape[0]
    want = max(CAST_BLOCK_ELEMS // cols, BF16_SUBLANES)
    tr = _pick(1 << (want.bit_length() - 1), rows)
    out = pl.pallas_call(
        _cast_kernel,
        out_shape=jax.ShapeDtypeStruct((rows, cols), BF16),
        grid=(rows // tr,),
        in_specs=[pl.BlockSpec((tr, cols), lambda i: (i, 0))],
        out_specs=pl.BlockSpec((tr, cols), lambda i: (i, 0)),
        compiler_params=_cparams(("arbitrary",)),
        name="cast_bf16",
    )(w2)
    return out.reshape(w.shape)


def _modulate_kernel(x_ref, shift_ref, scale_ref, o_ref):
    o_ref[...] = (x_ref[...] * (1.0 + scale_ref[...]) + shift_ref[...]).astype(BF16)


def _modulate(x, mods, layer, row_of_tile_fn, tm):
    rows, d = x.shape
    rot = row_of_tile_fn.mod_row(tm)
    return pl.pallas_call(
        _modulate_kernel,
        out_shape=jax.ShapeDtypeStruct((rows, d), BF16),
        grid=(rows // tm,),
        in_specs=[
            pl.BlockSpec((tm, d), lambda i: (i, 0)),
            _mod_spec(layer, 0, rot, d),
            _mod_spec(layer, 1, rot, d),
        ],
        out_specs=pl.BlockSpec((tm, d), lambda i: (i, 0)),
        compiler_params=_cparams(("arbitrary",)),
        name="modulate",
    )(x, mods, mods)


def _conv_in_kernel(*refs, n_jobs):
    a_ref, wb_ref, wc_ref, wv_ref = refs[:4]
    job_in = refs[4:4 + n_jobs]
    gb_ref, u_ref = refs[4 + n_jobs:6 + n_jobs]
    job_out = refs[6 + n_jobs:]
    a = a_ref[...]
    gb_ref[...] = _dot(a, wb_ref[...]).astype(BF16)
    u_ref[...] = (_dot(a, wc_ref[...]) * _dot(a, wv_ref[...])).astype(BF16)
    for x_ref, o_ref in zip(job_in, job_out):
        o_ref[...] = x_ref[...].astype(BF16)


def _conv_in(a, w_in, tile_limit, cast_jobs=()):
    rows, d = a.shape
    tm = _pick(1024, tile_limit)
    tn = _pick(256, d)
    nj, ni = d // tn, rows // tm
    jobs2d = [w.reshape(-1, w.shape[-1]) for w in cast_jobs]
    for w2 in jobs2d:
        assert w2.shape[0] % (nj * ni * BF16_SUBLANES) == 0, (w2.shape, nj, ni)
    job_specs = [pl.BlockSpec((w2.shape[0] // (nj * ni), w2.shape[1]), lambda j, i: (j * ni + i, 0)) for w2 in jobs2d]
    out = jax.ShapeDtypeStruct((rows, d), BF16)
    tile_spec = pl.BlockSpec((tm, tn), lambda j, i: (i, j))
    w_spec = lambda t: pl.BlockSpec((d, tn), lambda j, i: (0, j + t * nj))
    gb, u, *casted = pl.pallas_call(
        functools.partial(_conv_in_kernel, n_jobs=len(jobs2d)),
        out_shape=(out, out, *(jax.ShapeDtypeStruct(w2.shape, BF16) for w2 in jobs2d)),
        grid=(nj, ni),
        in_specs=[pl.BlockSpec((tm, d), lambda j, i: (i, 0)), w_spec(0), w_spec(1), w_spec(2), *job_specs],
        out_specs=(tile_spec, tile_spec, *job_specs),
        compiler_params=_cparams(("arbitrary", "arbitrary")),
        name="conv_in",
    )(a, w_in, w_in, w_in, *jobs2d)
    return gb, u, [c.reshape(w.shape) for c, w in zip(casted, cast_jobs)]


LN_TILE_ROWS = 1024
LN_ROW_CHUNK = 64
OUT_SLOTS = 2
ACC_COL_CHUNK = 1024
MLP_FF_TILE = 512
ATTN_OUT_K_TILE = 1024
CONV_OUT_K_TILE = 512


class _LnRefs(NamedTuple):
    hres: Any
    gate: Any
    g: Any
    b: Any
    nshift: Any
    nscale: Any
    h_hbm: Any
    an_hbm: Any
    acc: Any
    ybuf: Any
    ysem: Any
    anbuf: Any
    ansem: Any


def _split_refs(refs, has_next):
    if has_next:
        *head, hres, gate, g, b, nshift, nscale, h_hbm, an_hbm, acc, ybuf, ysem, anbuf, ansem = refs
        return head, _LnRefs(hres, gate, g, b, nshift, nscale, h_hbm, an_hbm, acc, ybuf, ysem, anbuf, ansem)
    *head, hres, gate, g, b, h_hbm, acc, ybuf, ysem = refs
    return head, _LnRefs(hres, gate, g, b, None, None, h_hbm, None, acc, ybuf, ysem, None, None)


def _zero_acc_at_first_step(ln, step):
    @pl.when(step == 0)
    def _():
        ln.acc[...] = jnp.zeros_like(ln.acc)


def _accumulate(ln, lhs, w_ref, alpha, step, n_steps):
    acc = ln.acc
    tm, d = acc.shape
    cw = _pick(ACC_COL_CHUNK, d)
    for n in range(d // cw):
        sl = slice(n * cw, (n + 1) * cw)
        acc[:, sl] += ln.gate[:, sl] * _dot(lhs, w_ref[:, sl])
    rc = tm // n_steps
    r0 = pl.multiple_of(step * rc, rc)
    acc[pl.ds(r0, rc), :] += alpha * ln.hres[...]


def _layer_norm_chunk(ln, slot, rows):
    d = ln.acc.shape[1]
    groups = [slice(t * LANES, (t + 1) * LANES) for t in range(d // LANES)]
    total = None
    for g in groups:
        total = ln.acc[rows, g] if total is None else total + ln.acc[rows, g]
    mu = jnp.sum(total, axis=-1, keepdims=True) / d
    total = None
    for g in groups:
        zc = ln.acc[rows, g] - mu
        total = zc * zc if total is None else total + zc * zc
    rstd = lax.rsqrt(jnp.sum(total, axis=-1, keepdims=True) / d + LN_EPS)
    for g in groups:
        y = (ln.acc[rows, g] - mu) * rstd * ln.g[:, g] + ln.b[:, g]
        ln.ybuf[slot, :, g] = y
        if ln.an_hbm is not None:
            ln.anbuf[slot, :, g] = (y * (1.0 + ln.nscale[:, g]) + ln.nshift[:, g]).astype(BF16)


def _layer_norm_rows(ln, row0):
    tm, d = ln.acc.shape
    rb = ln.ybuf.shape[1]

    def copies(slot, r0):
        dst = pl.ds(row0 + r0, rb)
        cs = [pltpu.make_async_copy(ln.ybuf.at[slot], ln.h_hbm.at[dst], ln.ysem.at[slot])]
        if ln.an_hbm is not None:
            cs.append(pltpu.make_async_copy(ln.anbuf.at[slot], ln.an_hbm.at[dst], ln.ansem.at[slot]))
        return cs

    def body(turn, carry):
        for slot in range(OUT_SLOTS):
            r0 = pl.multiple_of((turn * OUT_SLOTS + slot) * rb, rb)

            @pl.when(turn > 0)
            def _():
                for c in copies(slot, r0):
                    c.wait()

            _layer_norm_chunk(ln, slot, pl.ds(r0, rb))
            for c in copies(slot, r0):
                c.start()
        return carry

    lax.fori_loop(0, tm // (OUT_SLOTS * rb), body, 0)
    for slot in range(OUT_SLOTS):
        for c in copies(slot, 0):
            c.wait()


def _finish_rows(ln, step, n_steps):
    row0 = pl.program_id(0) * ln.acc.shape[0]

    @pl.when(step == n_steps - 1)
    def _():
        _layer_norm_rows(ln, row0)


def _mlp_kernel(*refs, alpha, n_steps, has_next):
    (a_ref, w1_ref, w2_ref), ln = _split_refs(refs, has_next)
    step = pl.program_id(1)
    _zero_acc_at_first_step(ln, step)
    hid = jnp.maximum(_dot(a_ref[...], w1_ref[...]), 0.0)
    _accumulate(ln, (hid * hid).astype(BF16), w2_ref, alpha, step, n_steps)
    _finish_rows(ln, step, n_steps)


def _plain_out_kernel(*refs, alpha, n_steps, has_next):
    (lhs_ref, w_ref), ln = _split_refs(refs, has_next)
    step = pl.program_id(1)
    _zero_acc_at_first_step(ln, step)
    _accumulate(ln, lhs_ref[...], w_ref, alpha, step, n_steps)
    _finish_rows(ln, step, n_steps)


def _conv_out_kernel(*refs, alpha, n_steps, has_next, seq_len):
    (gb_ref, u_ref, up_ref, un_ref, cw_ref, cb_ref, w_ref), ln = _split_refs(refs, has_next)
    step = pl.program_id(1)
    _zero_acc_at_first_step(ln, step)
    tm = u_ref.shape[0]
    u = u_ref[...].astype(F32)
    local = lax.broadcasted_iota(jnp.int32, (tm, 1), 0)
    pos = (local + pl.program_id(0) * tm) % seq_len
    prev_row = up_ref[BF16_SUBLANES - 1:BF16_SUBLANES, :].astype(F32)
    next_row = un_ref[0:1, :].astype(F32)
    u_prev = jnp.where(local == 0, prev_row, pltpu.roll(u, 1, axis=0))
    u_prev = jnp.where(pos == 0, 0.0, u_prev)
    u_next = jnp.where(local == tm - 1, next_row, pltpu.roll(u, tm - 1, axis=0))
    u_next = jnp.where(pos == seq_len - 1, 0.0, u_next)
    conv = u_prev * cw_ref[0:1, :] + u * cw_ref[1:2, :] + u_next * cw_ref[2:3, :] + cb_ref[...]
    lhs = (gb_ref[...].astype(F32) * conv).astype(BF16)
    _accumulate(ln, lhs, w_ref, alpha, step, n_steps)
    _finish_rows(ln, step, n_steps)


def _residual_ln_call(kernel, name, lead_specs, lead_args, hres, mods, layer, gate_idx, rot_fn,
                      ln_g, ln_b, next_mod, tm, n_steps, alpha, **kernel_kwargs):
    rows, d = hres.shape
    rc = tm // n_steps
    rb = _pick(LN_ROW_CHUNK, tm // OUT_SLOTS)
    assert rc * n_steps == tm and rc % 8 == 0 and tm % (OUT_SLOTS * rb) == 0, (tm, n_steps, rb)
    rot = rot_fn.mod_row(tm)
    in_specs = list(lead_specs) + [
        pl.BlockSpec((rc, d), lambda i, k: (i * n_steps + k, 0)),
        _mod_spec(layer, gate_idx, rot, d),
        _row_spec(d),
        _row_spec(d),
    ]
    args = list(lead_args) + [hres, mods, ln_g, ln_b]
    out_shape = [jax.ShapeDtypeStruct((rows, d), F32)]
    scratch = [pltpu.VMEM((tm, d), F32), pltpu.VMEM((OUT_SLOTS, rb, d), F32), pltpu.SemaphoreType.DMA((OUT_SLOTS,))]
    has_next = next_mod is not None
    if has_next:
        nl, nidx = next_mod
        in_specs += [_mod_spec(nl, nidx, rot, d), _mod_spec(nl, nidx + 1, rot, d)]
        args += [mods, mods]
        out_shape.append(jax.ShapeDtypeStruct((rows, d), BF16))
        scratch += [pltpu.VMEM((OUT_SLOTS, rb, d), BF16), pltpu.SemaphoreType.DMA((OUT_SLOTS,))]
    res = pl.pallas_call(
        functools.partial(kernel, alpha=alpha, n_steps=n_steps, has_next=has_next, **kernel_kwargs),
        out_shape=tuple(out_shape),
        grid=(rows // tm, n_steps),
        in_specs=in_specs,
        out_specs=tuple(pl.BlockSpec(memory_space=pl.ANY) for _ in out_shape),
        scratch_shapes=scratch,
        compiler_params=_cparams(("arbitrary", "arbitrary")),
        name=name,
    )(*args)
    return res if has_next else (res[0], None)


def _mlp(a, w1, w2, hres, mods, layer, rot_fn, ln_g, ln_b, next_mod, alpha):
    rows, d = a.shape
    d_ff = w1.shape[2]
    tm = _pick(LN_TILE_ROWS, rot_fn.tile_limit)
    tf = _pick(MLP_FF_TILE, d_ff)
    lead_specs = [
        pl.BlockSpec((tm, d), lambda i, k: (i, 0), pipeline_mode=pl.Buffered(1)),
        pl.BlockSpec((None, d, tf), lambda i, k: (layer, 0, k)),
        pl.BlockSpec((None, tf, d), lambda i, k: (layer, k, 0)),
    ]
    return _residual_ln_call(_mlp_kernel, "mlp", lead_specs, [a, w1, w2], hres, mods, layer, 5, rot_fn,
                             ln_g, ln_b, next_mod, tm, d_ff // tf, alpha)


def _plain_out(lhs, w, hres, mods, layer, rot_fn, ln_g, ln_b, next_mod, alpha):
    rows, kdim = lhs.shape
    d = w.shape[1]
    tm = _pick(LN_TILE_ROWS, rot_fn.tile_limit)
    tk = _pick(ATTN_OUT_K_TILE, kdim)
    lead_specs = [
        pl.BlockSpec((tm, tk), lambda i, k: (i, k)),
        pl.BlockSpec((tk, d), lambda i, k: (k, 0)),
    ]
    return _residual_ln_call(_plain_out_kernel, "attn_out", lead_specs, [lhs, w], hres, mods, layer, 2,
                             rot_fn, ln_g, ln_b, next_mod, tm, kdim // tk, alpha)


def _conv_out(gb, u, conv_w, conv_b, w, hres, mods, layer, rot_fn, ln_g, ln_b, next_mod, alpha, seq_len):
    rows, kdim = u.shape
    d = w.shape[1]
    tm = _pick(LN_TILE_ROWS, rot_fn.tile_limit)
    tk = _pick(CONV_OUT_K_TILE, kdim)
    hb = BF16_SUBLANES
    n_halo = rows // hb
    lead_specs = [
        pl.BlockSpec((tm, tk), lambda i, k: (i, k)),
        pl.BlockSpec((tm, tk), lambda i, k: (i, k)),
        pl.BlockSpec((hb, tk), lambda i, k: (jnp.maximum(i * (tm // hb) - 1, 0), k)),
        pl.BlockSpec((hb, tk), lambda i, k: (jnp.minimum((i + 1) * (tm // hb), n_halo - 1), k)),
        pl.BlockSpec((3, tk), lambda i, k: (0, k)),
        pl.BlockSpec((1, tk), lambda i, k: (0, k)),
        pl.BlockSpec((tk, d), lambda i, k: (k, 0)),
    ]
    return _residual_ln_call(_conv_out_kernel, "conv_out", lead_specs, [gb, u, u, u, conv_w, conv_b, w],
                             hres, mods, layer, 2, rot_fn, ln_g, ln_b, next_mod, tm, kdim // tk, alpha,
                             seq_len=seq_len)


def _rms_rope_heads(y, gain_ref, rope_refs, o_ref):
    tm, tn = y.shape
    if rope_refs is not None:
        cos_ref, sin_ref = rope_refs
        lane = lax.broadcasted_iota(jnp.int32, (tm, HEAD_DIM), 1)
        first_half = (lane % (HEAD_DIM // 2)) < (HEAD_DIM // 4)
    for h in range(tn // HEAD_DIM):
        sl = slice(h * HEAD_DIM, (h + 1) * HEAD_DIM)
        x = y[:, sl]
        x = x * lax.rsqrt(jnp.mean(x * x, axis=-1, keepdims=True) + RMS_EPS) * gain_ref[...]
        if rope_refs is not None:
            rot = jnp.where(first_half,
                            pltpu.roll(x, HEAD_DIM - HEAD_DIM // 4, axis=1),
                            pltpu.roll(x, HEAD_DIM // 4, axis=1))
            x = x * cos_ref[...] + rot * sin_ref[...]
        o_ref[:, sl] = x.astype(BF16)


def _qkv_kernel(*refs, n_q_tiles, n_k_tiles, rope):
    if rope:
        a_ref, w_ref, qg_ref, kg_ref, cos_ref, sin_ref, o_ref, ws, y0_ref, y1_ref = refs
        rope_refs = (cos_ref, sin_ref)
    else:
        a_ref, w_ref, qg_ref, kg_ref, o_ref, ws, y0_ref, y1_ref = refs
        rope_refs = None
    j, i = pl.program_id(0), pl.program_id(1)

    @pl.when(i == 0)
    def _():
        ws[...] = w_ref[...].astype(BF16)

    @pl.when((i == 0) & (j == 0))
    def _():
        y1_ref[...] = jnp.zeros_like(y1_ref)

    def plain(y):
        o_ref[...] = y.astype(BF16)

    kinds = [((j >= n_q_tiles) & (j < n_q_tiles + n_k_tiles), lambda y: _rms_rope_heads(y, kg_ref, rope_refs, o_ref)),
             (j >= n_q_tiles + n_k_tiles, plain)]
    if n_q_tiles:
        kinds.append((j < n_q_tiles, lambda y: _rms_rope_heads(y, qg_ref, rope_refs, o_ref)))
    for parity, (y_new, y_old) in enumerate([(y0_ref, y1_ref), (y1_ref, y0_ref)]):
        for is_kind, epilogue in kinds:
            @pl.when((i % 2 == parity) & is_kind)
            def _(y_new=y_new, y_old=y_old, epilogue=epilogue):
                y_new[...] = _dot(a_ref[...], ws[...])
                epilogue(y_old[...])


def _qkv(a, w_qkv, q_gain, k_gain, rope_tables, col0, q_dim, kv_dim, seq_len):
    rows, d = a.shape
    n_cols = w_qkv.shape[1] - col0
    tm = _pick(1024, seq_len if rope_tables is not None else rows)
    tn = _pick(512, kv_dim)
    assert col0 % tn == 0 and q_dim % tn == 0
    j0 = col0 // tn
    n_q_tiles = (q_dim - col0) // tn if col0 < q_dim else 0
    n_k_tiles = kv_dim // tn
    rope = rope_tables is not None
    n_row_tiles = rows // tm
    in_specs = [
        pl.BlockSpec((tm, d), lambda j, i: (jnp.minimum(i, n_row_tiles - 1), 0)),
        pl.BlockSpec((d, tn), lambda j, i: (0, j + j0)),
        pl.BlockSpec((1, HEAD_DIM), lambda j, i: (0, 0)),
        pl.BlockSpec((1, HEAD_DIM), lambda j, i: (0, 0)),
    ]
    args = [a, w_qkv, q_gain, k_gain]
    if rope:
        tiles_per_seq = seq_len // tm
        tab = pl.BlockSpec((tm, HEAD_DIM), lambda j, i: (jnp.maximum(i - 1, 0) % tiles_per_seq, 0))
        in_specs += [tab, tab]
        args += list(rope_tables)
    return pl.pallas_call(
        functools.partial(_qkv_kernel, n_q_tiles=n_q_tiles, n_k_tiles=n_k_tiles, rope=rope),
        out_shape=jax.ShapeDtypeStruct((rows, n_cols), BF16),
        grid=(n_cols // tn, n_row_tiles + 1),
        in_specs=in_specs,
        out_specs=pl.BlockSpec((tm, tn), lambda j, i: (jnp.maximum(i - 1, 0), j)),
        scratch_shapes=[pltpu.VMEM((d, tn), BF16), pltpu.VMEM((tm, tn), F32), pltpu.VMEM((tm, tn), F32)],
        compiler_params=_cparams(("arbitrary", "arbitrary")),
        name="qkv" if rope else "kv_ctx",
    )(*args)


def _rope_tables(seq_len):
    rows = seq_len // GRID_W
    row = jnp.repeat(jnp.arange(rows), GRID_W)
    col = jnp.tile(jnp.arange(GRID_W), rows)
    half = HEAD_DIM // 2
    inv_freq = ROPE_THETA ** (-jnp.arange(0, half, 2, dtype=F32) / half)

    def axis_angles(pos):
        ang = pos.astype(F32)[:, None] * inv_freq[None, :]
        return jnp.concatenate([ang, ang], axis=-1)

    ang = jnp.concatenate([axis_angles(row), axis_angles(col)], axis=-1)
    lane = jnp.arange(HEAD_DIM)
    sign = jnp.where((lane % half) < (HEAD_DIM // 4), -1.0, 1.0).astype(F32)
    return jnp.cos(ang), jnp.sin(ang) * sign[None, :]


_NT = (((1,), (1,)), ((), ()))
ATTN_KEY_CHUNK = 256
ATTN_Q_TILE = 256


_TN = (((0,), (0,)), ((), ()))
SUBLANES = 8
_DONE = object()


def _row_group_fold(x, op):
    n = x.shape[0] // SUBLANES
    return op(x.reshape(n, SUBLANES, x.shape[1]), axis=0)


def _score_tile(q_ref, kc_ref, kl_ref, s_ref, m_ref, key_chunk):
    ctx_len, seq_len = kc_ref.shape[0], kl_ref.shape[0]
    q = jnp.concatenate([q_ref[:, g * HEAD_DIM:(g + 1) * HEAD_DIM] for g in range(GQA_GROUP)], axis=0)
    segs = [(kc_ref, 0, ctx_len, 0)] + [(kl_ref, r, key_chunk, ctx_len + r) for r in range(0, seq_len, key_chunk)]
    m_rows = None
    for k_ref, r0, cnt, c0 in segs:
        st = lax.dot_general(k_ref[r0:r0 + cnt, :], q, _NT, preferred_element_type=F32)
        s_ref[c0:c0 + cnt, :] = st
        fold = _row_group_fold(st, jnp.max)
        m_rows = fold if m_rows is None else jnp.maximum(m_rows, fold)
        yield
    m_ref[...] = jnp.broadcast_to(jnp.max(m_rows, axis=0, keepdims=True), m_ref.shape)


def _finish_tile(s_ref, m_ref, vc_ref, vl_ref, o_ref, row0, scale, key_chunk):
    ctx_len, seq_len = vc_ref.shape[0], vl_ref.shape[0]
    rows = s_ref.shape[1]
    tq = rows // GQA_GROUP
    segs = [(vc_ref, 0, ctx_len, 0)] + [(vl_ref, r, key_chunk, ctx_len + r) for r in range(0, seq_len, key_chunk)]
    m = m_ref[0:1, :]
    l_rows = jnp.zeros((SUBLANES, rows), F32)
    o_t = jnp.zeros((HEAD_DIM, rows), F32)
    for v_ref, r0, cnt, c0 in segs:
        p_t = jnp.exp2((s_ref[c0:c0 + cnt, :] - m) * (scale * LOG2_E))
        l_rows = l_rows + _row_group_fold(p_t, jnp.sum)
        o_t = o_t + lax.dot_general(v_ref[r0:r0 + cnt, :], p_t.astype(BF16), _TN, preferred_element_type=F32)
        yield
    o = (o_t / jnp.sum(l_rows, axis=0, keepdims=True)).T
    for g in range(GQA_GROUP):
        o_ref[row0:row0 + tq, g * HEAD_DIM:(g + 1) * HEAD_DIM] = o[g * tq:(g + 1) * tq, :].astype(BF16)


def _attention_kernel(q_ref, kc_ref, kl_ref, vc_ref, vl_ref, o_ref, s0_ref, s1_ref, m0_ref, m1_ref,
                      *, scale, key_chunk):
    k = pl.program_id(0)

    @pl.when(k == 0)
    def _():
        s1_ref[...] = jnp.zeros_like(s1_ref)
        m1_ref[...] = jnp.zeros_like(m1_ref)

    def step(s_new, m_new, s_old, m_old):
        stages = [_finish_tile(s_old, m_old, vc_ref, vl_ref, o_ref, 0, scale, key_chunk),
                  _score_tile(q_ref, kc_ref, kl_ref, s_new, m_new, key_chunk)]
        while stages:
            stages = [stage for stage in stages if next(stage, _DONE) is not _DONE]

    @pl.when(k % 2 == 0)
    def _():
        step(s0_ref, m0_ref, s1_ref, m1_ref)

    @pl.when(k % 2 == 1)
    def _():
        step(s1_ref, m1_ref, s0_ref, m0_ref)


def _attention(qkv_lat, kv_ctx, bsz, seq_len, ctx_len, q_dim, kv_dim):
    n_kv = kv_dim // HEAD_DIM
    gw = GQA_GROUP * HEAD_DIM
    tq = _pick(ATTN_Q_TILE, seq_len)
    tiles = seq_len // tq
    n_tiles = bsz * n_kv * tiles
    k0 = q_dim // HEAD_DIM
    v0 = k0 + n_kv
    key_chunk = _pick(ATTN_KEY_CHUNK, seq_len)
    assert ctx_len % HEAD_DIM == 0 and key_chunk % HEAD_DIM == 0

    def tile_of(n):
        n = jnp.clip(n, 0, n_tiles - 1)
        return n // (n_kv * tiles), (n // tiles) % n_kv, n % tiles

    def at(offset, f):
        return lambda k: f(*tile_of(k + offset))

    rows = GQA_GROUP * tq
    return pl.pallas_call(
        functools.partial(_attention_kernel, scale=HEAD_DIM ** -0.5, key_chunk=key_chunk),
        out_shape=jax.ShapeDtypeStruct((bsz * seq_len, q_dim), BF16),
        grid=(n_tiles + 1,),
        in_specs=[
            pl.BlockSpec((tq, gw), at(0, lambda b, h, t: (b * tiles + t, h))),
            pl.BlockSpec((ctx_len, HEAD_DIM), at(0, lambda b, h, t: (b, h))),
            pl.BlockSpec((seq_len, HEAD_DIM), at(0, lambda b, h, t: (b, k0 + h))),
            pl.BlockSpec((ctx_len, HEAD_DIM), at(-1, lambda b, h, t: (b, n_kv + h))),
            pl.BlockSpec((seq_len, HEAD_DIM), at(-1, lambda b, h, t: (b, v0 + h))),
        ],
        out_specs=pl.BlockSpec((tq, gw), at(-1, lambda b, h, t: (b * tiles + t, h))),
        scratch_shapes=[pltpu.VMEM((ctx_len + seq_len, rows), F32), pltpu.VMEM((ctx_len + seq_len, rows), F32),
                        pltpu.VMEM((SUBLANES, rows), F32), pltpu.VMEM((SUBLANES, rows), F32)],
        compiler_params=_cparams(("arbitrary",)),
        name="attention",
    )(qkv_lat, kv_ctx, qkv_lat, kv_ctx, qkv_lat)


def kernel(x, c, ctx, c_ctx, ada_w, ada_b, ln_g, ln_b, mlp_w1, mlp_w2, conv_in_w, conv_w, conv_b, conv_out_w,
           attn_qkv_w, attn_q_gain, attn_k_gain, attn_out_w):
    bsz, seq_len, d = x.shape
    ctx_len = ctx.shape[1]
    depth = ada_w.shape[0]
    assert depth == 2 and bsz < COND_ROWS, "one conv-mixer layer followed by one attention layer"
    alpha = (2 * depth) ** 0.25
    kv_dim = (attn_qkv_w.shape[2] - d) // 2
    q_dim = d

    cond = jnp.zeros((COND_ROWS, d), F32).at[:bsz].set(c).at[bsz].set(c_ctx)
    mods = _adaln(cond, ada_w, ada_b).reshape(depth, COND_ROWS, N_MOD, 1, d)

    x_lat = x.reshape(bsz * seq_len, d)
    x_ctx = ctx.reshape(bsz * ctx_len, d)
    lat_rot = _RowMap(seq_len, lambda tm: (lambda i: i // (seq_len // tm)))
    ctx_rot = _RowMap(bsz * ctx_len, lambda tm: (lambda i: bsz))
    row = lambda v: v.reshape(1, -1)

    conv_in_wb = _to_bf16(conv_in_w)[0]

    def layer0(h, rot_fn, sub_len, final_next_mod, weights_bf16):
        a = _modulate(h, mods, 0, rot_fn, _pick(512, rot_fn.tile_limit))
        if weights_bf16 is None:
            gb, u, weights_bf16 = _conv_in(a, conv_in_wb, rot_fn.tile_limit,
                                           (mlp_w1, mlp_w2, conv_out_w, attn_out_w))
        else:
            gb, u, _ = _conv_in(a, conv_in_wb, rot_fn.tile_limit)
        w1b, w2b, conv_out_wb, _ = weights_bf16
        h1, a1 = _conv_out(gb, u, conv_w[0], row(conv_b[0]), conv_out_wb[0], h, mods, 0, rot_fn,
                           row(ln_g[0, 0]), row(ln_b[0, 0]), (0, 3), alpha, sub_len)
        h2, a2 = _mlp(a1, w1b, w2b, h1, mods, 0, rot_fn, row(ln_g[0, 1]), row(ln_b[0, 1]),
                      final_next_mod, alpha)
        return h2, a2, weights_bf16

    h_lat, a_lat, weights_bf16 = layer0(x_lat, lat_rot, seq_len, (1, 0), None)
    _, a_ctx, _ = layer0(x_ctx, ctx_rot, ctx_len, (1, 0), weights_bf16)
    w1b, w2b, _, attn_out_wb = weights_bf16

    cos, sin_signed = _rope_tables(seq_len)
    qg, kg = row(attn_q_gain[0]), row(attn_k_gain[0])
    qkv_lat = _qkv(a_lat, attn_qkv_w[0], qg, kg, (cos, sin_signed), 0, q_dim, kv_dim, seq_len)
    kv_ctx = _qkv(a_ctx, attn_qkv_w[0], qg, kg, None, q_dim, q_dim, kv_dim, ctx_len)
    o = _attention(qkv_lat, kv_ctx, bsz, seq_len, ctx_len, q_dim, kv_dim)
    h_lat, a_lat = _plain_out(o, attn_out_wb[0], h_lat, mods, 1, lat_rot, row(ln_g[1, 0]), row(ln_b[1, 0]),
                              (1, 3), alpha)
    h_lat, _ = _mlp(a_lat, w1b, w2b, h_lat, mods, 1, lat_rot, row(ln_g[1, 1]), row(ln_b[1, 1]),
                    None, alpha)
    return h_lat.reshape(bsz, seq_len, d)
```

```python
import functools
from typing import Any, Callable, NamedTuple

import jax
import jax.numpy as jnp
from jax import lax
from jax.experimental import pallas as pl
from jax.experimental.pallas import tpu as pltpu

F32 = jnp.float32
BF16 = jnp.bfloat16

HEAD_DIM = 128
GQA_GROUP = 4
GRID_W = 64
N_MOD = 6
ROPE_THETA = 10000.0
LN_EPS = 1e-5
RMS_EPS = 1e-6
LOG2_E = 1.4426950408889634
COND_ROWS = 8
BF16_SUBLANES = 16
LANES = 128
VMEM_LIMIT = 60 * 1024 * 1024


class _RowMap(NamedTuple):
    tile_limit: int
    mod_row: Callable


def _cparams(sem):
    return pltpu.CompilerParams(dimension_semantics=sem, vmem_limit_bytes=VMEM_LIMIT)


def _pick(pref, n):
    t = min(pref, n)
    while n % t:
        t //= 2
    return t


def _dot(a, b):
    return jnp.dot(a, b, preferred_element_type=F32)


def _adaln_kernel(cond_ref, w_ref, b_ref, o_ref):
    c = cond_ref[...]
    s = (c * jax.nn.sigmoid(c)).astype(BF16)
    o_ref[...] = _dot(s, w_ref[...].astype(BF16)) + b_ref[...]


def _adaln(cond, ada_w, ada_b):
    depth, d, n = ada_w.shape
    tn = _pick(512, n)
    return pl.pallas_call(
        _adaln_kernel,
        out_shape=jax.ShapeDtypeStruct((depth, COND_ROWS, n), F32),
        grid=(depth, n // tn),
        in_specs=[
            pl.BlockSpec((COND_ROWS, d), lambda l, j: (0, 0)),
            pl.BlockSpec((None, d, tn), lambda l, j: (l, 0, j)),
            pl.BlockSpec((None, 1, tn), lambda l, j: (l, 0, j)),
        ],
        out_specs=pl.BlockSpec((None, COND_ROWS, tn), lambda l, j: (l, 0, j)),
        compiler_params=_cparams(("arbitrary", "arbitrary")),
        name="adaln",
    )(cond, ada_w, ada_b.reshape(depth, 1, n))


def _mod_spec(layer, which, row_of_tile, d):
    return pl.BlockSpec((None, None, None, 1, d),
                        lambda i, *_: (layer, row_of_tile(i), which, 0, 0))


def _row_spec(d):
    return pl.BlockSpec((1, d), lambda *_: (0, 0))


CAST_BLOCK_ELEMS = 2 * 1024 * 1024


def _cast_kernel(x_ref, o_ref):
    o_ref[...] = x_ref[...].astype(BF16)


def _to_bf16(w):
    cols = w.shape[-1]
    w2 = w.reshape(-1, cols)
    rows = w2.shape[0]
    want = max(CAST_BLOCK_ELEMS // cols, BF16_SUBLANES)
    tr = _pick(1 << (want.bit_length() - 1), rows)
    out = pl.pallas_call(
        _cast_kernel,
        out_shape=jax.ShapeDtypeStruct((rows, cols), BF16),
        grid=(rows // tr,),
        in_specs=[pl.BlockSpec((tr, cols), lambda i: (i, 0))],
        out_specs=pl.BlockSpec((tr, cols), lambda i: (i, 0)),
        compiler_params=_cparams(("arbitrary",)),
        name="cast_bf16",
    )(w2)
    return out.reshape(w.shape)


def _modulate_kernel(x_ref, shift_ref, scale_ref, o_ref):
    o_ref[...] = (x_ref[...] * (1.0 + scale_ref[...]) + shift_ref[...]).astype(BF16)


def _modulate(x, mods, layer, row_of_tile_fn, tm):
    rows, d = x.shape
    rot = row_of_tile_fn.mod_row(tm)
    return pl.pallas_call(
        _modulate_kernel,
        out_shape=jax.ShapeDtypeStruct((rows, d), BF16),
        grid=(rows // tm,),
        in_specs=[
            pl.BlockSpec((tm, d), lambda i: (i, 0)),
            _mod_spec(layer, 0, rot, d),
            _mod_spec(layer, 1, rot, d),
        ],
        out_specs=pl.BlockSpec((tm, d), lambda i: (i, 0)),
        compiler_params=_cparams(("arbitrary",)),
        name="modulate",
    )(x, mods, mods)


def _conv_in_kernel(*refs, n_jobs):
    a_ref, wb_ref, wc_ref, wv_ref = refs[:4]
    job_in = refs[4:4 + n_jobs]
    gb_ref, u_ref = refs[4 + n_jobs:6 + n_jobs]
    job_out = refs[6 + n_jobs:]
    a = a_ref[...]
    gb_ref[...] = _dot(a, wb_ref[...]).astype(BF16)
    u_ref[...] = (_dot(a, wc_ref[...]) * _dot(a, wv_ref[...])).astype(BF16)
    for x_ref, o_ref in zip(job_in, job_out):
        o_ref[...] = x_ref[...].astype(BF16)


def _conv_in(a, w_in, tile_limit, cast_jobs=()):
    rows, d = a.shape
    tm = _pick(1024, tile_limit)
    tn = _pick(256, d)
    nj, ni = d // tn, rows // tm
    jobs2d = [w.reshape(-1, w.shape[-1]) for w in cast_jobs]
    for w2 in jobs2d:
        assert w2.shape[0] % (nj * ni * BF16_SUBLANES) == 0, (w2.shape, nj, ni)
    job_specs = [pl.BlockSpec((w2.shape[0] // (nj * ni), w2.shape[1]), lambda j, i: (j * ni + i, 0)) for w2 in jobs2d]
    out = jax.ShapeDtypeStruct((rows, d), BF16)
    tile_spec = pl.BlockSpec((tm, tn), lambda j, i: (i, j))
    w_spec = lambda t: pl.BlockSpec((d, tn), lambda j, i: (0, j + t * nj))
    gb, u, *casted = pl.pallas_call(
        functools.partial(_conv_in_kernel, n_jobs=len(jobs2d)),
        out_shape=(out, out, *(jax.ShapeDtypeStruct(w2.shape, BF16) for w2 in jobs2d)),
        grid=(nj, ni),
        in_specs=[pl.BlockSpec((tm, d), lambda j, i: (i, 0)), w_spec(0), w_spec(1), w_spec(2), *job_specs],
        out_specs=(tile_spec, tile_spec, *job_specs),
        compiler_params=_cparams(("arbitrary", "arbitrary")),
        name="conv_in",
    )(a, w_in, w_in, w_in, *jobs2d)
    return gb, u, [c.reshape(w.shape) for c, w in zip(casted, cast_jobs)]


LN_TILE_ROWS = 1024
LN_ROW_CHUNK = 64
OUT_SLOTS = 2
ACC_COL_CHUNK = 1024
MLP_FF_TILE = 512
ATTN_OUT_K_TILE = 1024
CONV_OUT_K_TILE = 512


class _LnRefs(NamedTuple):
    hres: Any
    gate: Any
    g: Any
    b: Any
    nshift: Any
    nscale: Any
    h_hbm: Any
    an_hbm: Any
    acc: Any
    ybuf: Any
    ysem: Any
    anbuf: Any
    ansem: Any


def _split_refs(refs, has_next):
    if has_next:
        *head, hres, gate, g, b, nshift, nscale, h_hbm, an_hbm, acc, ybuf, ysem, anbuf, ansem = refs
        return head, _LnRefs(hres, gate, g, b, nshift, nscale, h_hbm, an_hbm, acc, ybuf, ysem, anbuf, ansem)
    *head, hres, gate, g, b, h_hbm, acc, ybuf, ysem = refs
    return head, _LnRefs(hres, gate, g, b, None, None, h_hbm, None, acc, ybuf, ysem, None, None)


def _zero_acc_at_first_step(ln, step):
    @pl.when(step == 0)
    def _():
        ln.acc[...] = jnp.zeros_like(ln.acc)


def _accumulate(ln, lhs, w_ref, alpha, step, n_steps):
    acc = ln.acc
    tm, d = acc.shape
    cw = _pick(ACC_COL_CHUNK, d)
    for n in range(d // cw):
        sl = slice(n * cw, (n + 1) * cw)
        acc[:, sl] += ln.gate[:, sl] * _dot(lhs, w_ref[:, sl])
    rc = tm // n_steps
    r0 = pl.multiple_of(step * rc, rc)
    acc[pl.ds(r0, rc), :] += alpha * ln.hres[...]


def _layer_norm_chunk(ln, slot, rows):
    d = ln.acc.shape[1]
    groups = [slice(t * LANES, (t + 1) * LANES) for t in range(d // LANES)]
    total = None
    for g in groups:
        total = ln.acc[rows, g] if total is None else total + ln.acc[rows, g]
    mu = jnp.sum(total, axis=-1, keepdims=True) / d
    total = None
    for g in groups:
        zc = ln.acc[rows, g] - mu
        total = zc * zc if total is None else total + zc * zc
    rstd = lax.rsqrt(jnp.sum(total, axis=-1, keepdims=True) / d + LN_EPS)
    for g in groups:
        y = (ln.acc[rows, g] - mu) * rstd * ln.g[:, g] + ln.b[:, g]
        ln.ybuf[slot, :, g] = y
        if ln.an_hbm is not None:
            ln.anbuf[slot, :, g] = (y * (1.0 + ln.nscale[:, g]) + ln.nshift[:, g]).astype(BF16)


def _layer_norm_rows(ln, row0):
    tm, d = ln.acc.shape
    rb = ln.ybuf.shape[1]

    def copies(slot, r0):
        dst = pl.ds(row0 + r0, rb)
        cs = [pltpu.make_async_copy(ln.ybuf.at[slot], ln.h_hbm.at[dst], ln.ysem.at[slot])]
        if ln.an_hbm is not None:
            cs.append(pltpu.make_async_copy(ln.anbuf.at[slot], ln.an_hbm.at[dst], ln.ansem.at[slot]))
        return cs

    def body(turn, carry):
        for slot in range(OUT_SLOTS):
            r0 = pl.multiple_of((turn * OUT_SLOTS + slot) * rb, rb)

            @pl.when(turn > 0)
            def _():
                for c in copies(slot, r0):
                    c.wait()

            _layer_norm_chunk(ln, slot, pl.ds(r0, rb))
            for c in copies(slot, r0):
                c.start()
        return carry

    lax.fori_loop(0, tm // (OUT_SLOTS * rb), body, 0)
    for slot in range(OUT_SLOTS):
        for c in copies(slot, 0):
            c.wait()


def _finish_rows(ln, step, n_steps):
    row0 = pl.program_id(0) * ln.acc.shape[0]

    @pl.when(step == n_steps - 1)
    def _():
        _layer_norm_rows(ln, row0)


def _mlp_kernel(*refs, alpha, n_steps, has_next):
    (a_ref, w1_ref, w2_ref), ln = _split_refs(refs, has_next)
    step = pl.program_id(1)
    _zero_acc_at_first_step(ln, step)
    hid = jnp.maximum(_dot(a_ref[...], w1_ref[...]), 0.0)
    _accumulate(ln, (hid * hid).astype(BF16), w2_ref, alpha, step, n_steps)
    _finish_rows(ln, step, n_steps)


def _plain_out_kernel(*refs, alpha, n_steps, has_next):
    (lhs_ref, w_ref), ln = _split_refs(refs, has_next)
    step = pl.program_id(1)
    _zero_acc_at_first_step(ln, step)
    _accumulate(ln, lhs_ref[...], w_ref, alpha, step, n_steps)
    _finish_rows(ln, step, n_steps)


def _conv_out_kernel(*refs, alpha, n_steps, has_next, seq_len):
    (gb_ref, u_ref, up_ref, un_ref, cw_ref, cb_ref, w_ref), ln = _split_refs(refs, has_next)
    step = pl.program_id(1)
    _zero_acc_at_first_step(ln, step)
    tm = u_ref.shape[0]
    u = u_ref[...].astype(F32)
    local = lax.broadcasted_iota(jnp.int32, (tm, 1), 0)
    pos = (local + pl.program_id(0) * tm) % seq_len
    prev_row = up_ref[BF16_SUBLANES - 1:BF16_SUBLANES, :].astype(F32)
    next_row = un_ref[0:1, :].astype(F32)
    u_prev = jnp.where(local == 0, prev_row, pltpu.roll(u, 1, axis=0))
    u_prev = jnp.where(pos == 0, 0.0, u_prev)
    u_next = jnp.where(local == tm - 1, next_row, pltpu.roll(u, tm - 1, axis=0))
    u_next = jnp.where(pos == seq_len - 1, 0.0, u_next)
    conv = u_prev * cw_ref[0:1, :] + u * cw_ref[1:2, :] + u_next * cw_ref[2:3, :] + cb_ref[...]
    lhs = (gb_ref[...].astype(F32) * conv).astype(BF16)
    _accumulate(ln, lhs, w_ref, alpha, step, n_steps)
    _finish_rows(ln, step, n_steps)


def _residual_ln_call(kernel, name, lead_specs, lead_args, hres, mods, layer, gate_idx, rot_fn,
                      ln_g, ln_b, next_mod, tm, n_steps, alpha, **kernel_kwargs):
    rows, d = hres.shape
    rc = tm // n_steps
    rb = _pick(LN_ROW_CHUNK, tm // OUT_SLOTS)
    assert rc * n_steps == tm and rc % 8 == 0 and tm % (OUT_SLOTS * rb) == 0, (tm, n_steps, rb)
    rot = rot_fn.mod_row(tm)
    in_specs = list(lead_specs) + [
        pl.BlockSpec((rc, d), lambda i, k: (i * n_steps + k, 0)),
        _mod_spec(layer, gate_idx, rot, d),
        _row_spec(d),
        _row_spec(d),
    ]
    args = list(lead_args) + [hres, mods, ln_g, ln_b]
    out_shape = [jax.ShapeDtypeStruct((rows, d), F32)]
    scratch = [pltpu.VMEM((tm, d), F32), pltpu.VMEM((OUT_SLOTS, rb, d), F32), pltpu.SemaphoreType.DMA((OUT_SLOTS,))]
    has_next = next_mod is not None
    if has_next:
        nl, nidx = next_mod
        in_specs += [_mod_spec(nl, nidx, rot, d), _mod_spec(nl, nidx + 1, rot, d)]
        args += [mods, mods]
        out_shape.append(jax.ShapeDtypeStruct((rows, d), BF16))
        scratch += [pltpu.VMEM((OUT_SLOTS, rb, d), BF16), pltpu.SemaphoreType.DMA((OUT_SLOTS,))]
    res = pl.pallas_call(
        functools.partial(kernel, alpha=alpha, n_steps=n_steps, has_next=has_next, **kernel_kwargs),
        out_shape=tuple(out_shape),
        grid=(rows // tm, n_steps),
        in_specs=in_specs,
        out_specs=tuple(pl.BlockSpec(memory_space=pl.ANY) for _ in out_shape),
        scratch_shapes=scratch,
        compiler_params=_cparams(("arbitrary", "arbitrary")),
        name=name,
    )(*args)
    return res if has_next else (res[0], None)


def _mlp(a, w1, w2, hres, mods, layer, rot_fn, ln_g, ln_b, next_mod, alpha):
    rows, d = a.shape
    d_ff = w1.shape[2]
    tm = _pick(LN_TILE_ROWS, rot_fn.tile_limit)
    tf = _pick(MLP_FF_TILE, d_ff)
    lead_specs = [
        pl.BlockSpec((tm, d), lambda i, k: (i, 0)),
        pl.BlockSpec((None, d, tf), lambda i, k: (layer, 0, k)),
        pl.BlockSpec((None, tf, d), lambda i, k: (layer, k, 0)),
    ]
    return _residual_ln_call(_mlp_kernel, "mlp", lead_specs, [a, w1, w2], hres, mods, layer, 5, rot_fn,
                             ln_g, ln_b, next_mod, tm, d_ff // tf, alpha)


def _plain_out(lhs, w, hres, mods, layer, rot_fn, ln_g, ln_b, next_mod, alpha):
    rows, kdim = lhs.shape
    d = w.shape[1]
    tm = _pick(LN_TILE_ROWS, rot_fn.tile_limit)
    tk = _pick(ATTN_OUT_K_TILE, kdim)
    lead_specs = [
        pl.BlockSpec((tm, tk), lambda i, k: (i, k)),
        pl.BlockSpec((tk, d), lambda i, k: (k, 0)),
    ]
    return _residual_ln_call(_plain_out_kernel, "attn_out", lead_specs, [lhs, w], hres, mods, layer, 2,
                             rot_fn, ln_g, ln_b, next_mod, tm, kdim // tk, alpha)


def _conv_out(gb, u, conv_w, conv_b, w, hres, mods, layer, rot_fn, ln_g, ln_b, next_mod, alpha, seq_len):
    rows, kdim = u.shape
    d = w.shape[1]
    tm = _pick(LN_TILE_ROWS, rot_fn.tile_limit)
    tk = _pick(CONV_OUT_K_TILE, kdim)
    hb = BF16_SUBLANES
    n_halo = rows // hb
    lead_specs = [
        pl.BlockSpec((tm, tk), lambda i, k: (i, k)),
        pl.BlockSpec((tm, tk), lambda i, k: (i, k)),
        pl.BlockSpec((hb, tk), lambda i, k: (jnp.maximum(i * (tm // hb) - 1, 0), k)),
        pl.BlockSpec((hb, tk), lambda i, k: (jnp.minimum((i + 1) * (tm // hb), n_halo - 1), k)),
        pl.BlockSpec((3, tk), lambda i, k: (0, k)),
        pl.BlockSpec((1, tk), lambda i, k: (0, k)),
        pl.BlockSpec((tk, d), lambda i, k: (k, 0)),
    ]
    return _residual_ln_call(_conv_out_kernel, "conv_out", lead_specs, [gb, u, u, u, conv_w, conv_b, w],
                             hres, mods, layer, 2, rot_fn, ln_g, ln_b, next_mod, tm, kdim // tk, alpha,
                             seq_len=seq_len)


def _rms_rope_heads(y, gain_ref, rope_refs, o_ref):
    tm, tn = y.shape
    if rope_refs is not None:
        cos_ref, sin_ref = rope_refs
        lane = lax.broadcasted_iota(jnp.int32, (tm, HEAD_DIM), 1)
        first_half = (lane % (HEAD_DIM // 2)) < (HEAD_DIM // 4)
    for h in range(tn // HEAD_DIM):
        sl = slice(h * HEAD_DIM, (h + 1) * HEAD_DIM)
        x = y[:, sl]
        x = x * lax.rsqrt(jnp.mean(x * x, axis=-1, keepdims=True) + RMS_EPS) * gain_ref[...]
        if rope_refs is not None:
            rot = jnp.where(first_half,
                            pltpu.roll(x, HEAD_DIM - HEAD_DIM // 4, axis=1),
                            pltpu.roll(x, HEAD_DIM // 4, axis=1))
            x = x * cos_ref[...] + rot * sin_ref[...]
        o_ref[:, sl] = x.astype(BF16)


def _qkv_kernel(*refs, n_q_tiles, n_k_tiles, rope):
    if rope:
        a_ref, w_ref, qg_ref, kg_ref, cos_ref, sin_ref, o_ref, ws, y0_ref, y1_ref = refs
        rope_refs = (cos_ref, sin_ref)
    else:
        a_ref, w_ref, qg_ref, kg_ref, o_ref, ws, y0_ref, y1_ref = refs
        rope_refs = None
    j, i = pl.program_id(0), pl.program_id(1)

    @pl.when(i == 0)
    def _():
        ws[...] = w_ref[...].astype(BF16)

    @pl.when((i == 0) & (j == 0))
    def _():
        y1_ref[...] = jnp.zeros_like(y1_ref)

    def plain(y):
        o_ref[...] = y.astype(BF16)

    kinds = [((j >= n_q_tiles) & (j < n_q_tiles + n_k_tiles), lambda y: _rms_rope_heads(y, kg_ref, rope_refs, o_ref)),
             (j >= n_q_tiles + n_k_tiles, plain)]
    if n_q_tiles:
        kinds.append((j < n_q_tiles, lambda y: _rms_rope_heads(y, qg_ref, rope_refs, o_ref)))
    for parity, (y_new, y_old) in enumerate([(y0_ref, y1_ref), (y1_ref, y0_ref)]):
        for is_kind, epilogue in kinds:
            @pl.when((i % 2 == parity) & is_kind)
            def _(y_new=y_new, y_old=y_old, epilogue=epilogue):
                y_new[...] = _dot(a_ref[...], ws[...])
                epilogue(y_old[...])


def _qkv(a, w_qkv, q_gain, k_gain, rope_tables, col0, q_dim, kv_dim, seq_len):
    rows, d = a.shape
    n_cols = w_qkv.shape[1] - col0
    tm = _pick(1024, seq_len if rope_tables is not None else rows)
    tn = _pick(512, kv_dim)
    assert col0 % tn == 0 and q_dim % tn == 0
    j0 = col0 // tn
    n_q_tiles = (q_dim - col0) // tn if col0 < q_dim else 0
    n_k_tiles = kv_dim // tn
    rope = rope_tables is not None
    n_row_tiles = rows // tm
    in_specs = [
        pl.BlockSpec((tm, d), lambda j, i: (jnp.minimum(i, n_row_tiles - 1), 0)),
        pl.BlockSpec((d, tn), lambda j, i: (0, j + j0)),
        pl.BlockSpec((1, HEAD_DIM), lambda j, i: (0, 0)),
        pl.BlockSpec((1, HEAD_DIM), lambda j, i: (0, 0)),
    ]
    args = [a, w_qkv, q_gain, k_gain]
    if rope:
        tiles_per_seq = seq_len // tm
        tab = pl.BlockSpec((tm, HEAD_DIM), lambda j, i: (jnp.maximum(i - 1, 0) % tiles_per_seq, 0))
        in_specs += [tab, tab]
        args += list(rope_tables)
    return pl.pallas_call(
        functools.partial(_qkv_kernel, n_q_tiles=n_q_tiles, n_k_tiles=n_k_tiles, rope=rope),
        out_shape=jax.ShapeDtypeStruct((rows, n_cols), BF16),
        grid=(n_cols // tn, n_row_tiles + 1),
        in_specs=in_specs,
        out_specs=pl.BlockSpec((tm, tn), lambda j, i: (jnp.maximum(i - 1, 0), j)),
        scratch_shapes=[pltpu.VMEM((d, tn), BF16), pltpu.VMEM((tm, tn), F32), pltpu.VMEM((tm, tn), F32)],
        compiler_params=_cparams(("arbitrary", "arbitrary")),
        name="qkv" if rope else "kv_ctx",
    )(*args)


def _rope_tables(seq_len):
    rows = seq_len // GRID_W
    row = jnp.repeat(jnp.arange(rows), GRID_W)
    col = jnp.tile(jnp.arange(GRID_W), rows)
    half = HEAD_DIM // 2
    inv_freq = ROPE_THETA ** (-jnp.arange(0, half, 2, dtype=F32) / half)

    def axis_angles(pos):
        ang = pos.astype(F32)[:, None] * inv_freq[None, :]
        return jnp.concatenate([ang, ang], axis=-1)

    ang = jnp.concatenate([axis_angles(row), axis_angles(col)], axis=-1)
    lane = jnp.arange(HEAD_DIM)
    sign = jnp.where((lane % half) < (HEAD_DIM // 4), -1.0, 1.0).astype(F32)
    return jnp.cos(ang), jnp.sin(ang) * sign[None, :]


_NT = (((1,), (1,)), ((), ()))
ATTN_KEY_CHUNK = 256
ATTN_Q_TILE = 256


_TN = (((0,), (0,)), ((), ()))
SUBLANES = 8
_DONE = object()


def _row_group_fold(x, op):
    n = x.shape[0] // SUBLANES
    return op(x.reshape(n, SUBLANES, x.shape[1]), axis=0)


def _score_tile(q_ref, kc_ref, kl_ref, s_ref, m_ref, key_chunk):
    ctx_len, seq_len = kc_ref.shape[0], kl_ref.shape[0]
    q = jnp.concatenate([q_ref[:, g * HEAD_DIM:(g + 1) * HEAD_DIM] for g in range(GQA_GROUP)], axis=0)
    segs = [(kc_ref, 0, ctx_len, 0)] + [(kl_ref, r, key_chunk, ctx_len + r) for r in range(0, seq_len, key_chunk)]
    m_rows = None
    for k_ref, r0, cnt, c0 in segs:
        st = lax.dot_general(k_ref[r0:r0 + cnt, :], q, _NT, preferred_element_type=F32)
        s_ref[c0:c0 + cnt, :] = st
        fold = _row_group_fold(st, jnp.max)
        m_rows = fold if m_rows is None else jnp.maximum(m_rows, fold)
        yield
    m_ref[...] = jnp.broadcast_to(jnp.max(m_rows, axis=0, keepdims=True), m_ref.shape)


def _finish_tile(s_ref, m_ref, vc_ref, vl_ref, o_ref, row0, scale, key_chunk):
    ctx_len, seq_len = vc_ref.shape[0], vl_ref.shape[0]
    rows = s_ref.shape[1]
    tq = rows // GQA_GROUP
    segs = [(vc_ref, 0, ctx_len, 0)] + [(vl_ref, r, key_chunk, ctx_len + r) for r in range(0, seq_len, key_chunk)]
    m = m_ref[0:1, :]
    l_rows = jnp.zeros((SUBLANES, rows), F32)
    o_t = jnp.zeros((HEAD_DIM, rows), F32)
    for v_ref, r0, cnt, c0 in segs:
        p_t = jnp.exp2((s_ref[c0:c0 + cnt, :] - m) * (scale * LOG2_E))
        l_rows = l_rows + _row_group_fold(p_t, jnp.sum)
        o_t = o_t + lax.dot_general(v_ref[r0:r0 + cnt, :], p_t.astype(BF16), _TN, preferred_element_type=F32)
        yield
    o = (o_t / jnp.sum(l_rows, axis=0, keepdims=True)).T
    for g in range(GQA_GROUP):
        o_ref[row0:row0 + tq, g * HEAD_DIM:(g + 1) * HEAD_DIM] = o[g * tq:(g + 1) * tq, :].astype(BF16)


def _attention_kernel(q_ref, kc_ref, kl_ref, vc_ref, vl_ref, o_ref, s0_ref, s1_ref, m0_ref, m1_ref,
                      *, scale, key_chunk):
    k = pl.program_id(0)

    @pl.when(k == 0)
    def _():
        s1_ref[...] = jnp.zeros_like(s1_ref)
        m1_ref[...] = jnp.zeros_like(m1_ref)

    def step(s_new, m_new, s_old, m_old):
        stages = [_finish_tile(s_old, m_old, vc_ref, vl_ref, o_ref, 0, scale, key_chunk),
                  _score_tile(q_ref, kc_ref, kl_ref, s_new, m_new, key_chunk)]
        while stages:
            stages = [stage for stage in stages if next(stage, _DONE) is not _DONE]

    @pl.when(k % 2 == 0)
    def _():
        step(s0_ref, m0_ref, s1_ref, m1_ref)

    @pl.when(k % 2 == 1)
    def _():
        step(s1_ref, m1_ref, s0_ref, m0_ref)


def _attention(qkv_lat, kv_ctx, bsz, seq_len, ctx_len, q_dim, kv_dim):
    n_kv = kv_dim // HEAD_DIM
    gw = GQA_GROUP * HEAD_DIM
    tq = _pick(ATTN_Q_TILE, seq_len)
    tiles = seq_len // tq
    n_tiles = bsz * n_kv * tiles
    k0 = q_dim // HEAD_DIM
    v0 = k0 + n_kv
    key_chunk = _pick(ATTN_KEY_CHUNK, seq_len)
    assert ctx_len % HEAD_DIM == 0 and key_chunk % HEAD_DIM == 0

    def tile_of(n):
        n = jnp.clip(n, 0, n_tiles - 1)
        return n // (n_kv * tiles), (n // tiles) % n_kv, n % tiles

    def at(offset, f):
        return lambda k: f(*tile_of(k + offset))

    rows = GQA_GROUP * tq
    return pl.pallas_call(
        functools.partial(_attention_kernel, scale=HEAD_DIM ** -0.5, key_chunk=key_chunk),
        out_shape=jax.ShapeDtypeStruct((bsz * seq_len, q_dim), BF16),
        grid=(n_tiles + 1,),
        in_specs=[
            pl.BlockSpec((tq, gw), at(0, lambda b, h, t: (b * tiles + t, h))),
            pl.BlockSpec((ctx_len, HEAD_DIM), at(0, lambda b, h, t: (b, h))),
            pl.BlockSpec((seq_len, HEAD_DIM), at(0, lambda b, h, t: (b, k0 + h))),
            pl.BlockSpec((ctx_len, HEAD_DIM), at(-1, lambda b, h, t: (b, n_kv + h))),
            pl.BlockSpec((seq_len, HEAD_DIM), at(-1, lambda b, h, t: (b, v0 + h))),
        ],
        out_specs=pl.BlockSpec((tq, gw), at(-1, lambda b, h, t: (b * tiles + t, h))),
        scratch_shapes=[pltpu.VMEM((ctx_len + seq_len, rows), F32), pltpu.VMEM((ctx_len + seq_len, rows), F32),
                        pltpu.VMEM((SUBLANES, rows), F32), pltpu.VMEM((SUBLANES, rows), F32)],
        compiler_params=_cparams(("arbitrary",)),
        name="attention",
    )(qkv_lat, kv_ctx, qkv_lat, kv_ctx, qkv_lat)


def kernel(x, c, ctx, c_ctx, ada_w, ada_b, ln_g, ln_b, mlp_w1, mlp_w2, conv_in_w, conv_w, conv_b, conv_out_w,
           attn_qkv_w, attn_q_gain, attn_k_gain, attn_out_w):
    bsz, seq_len, d = x.shape
    ctx_len = ctx.shape[1]
    depth = ada_w.shape[0]
    assert depth == 2 and bsz < COND_ROWS, "one conv-mixer layer followed by one attention layer"
    alpha = (2 * depth) ** 0.25
    kv_dim = (attn_qkv_w.shape[2] - d) // 2
    q_dim = d

    cond = jnp.zeros((COND_ROWS, d), F32).at[:bsz].set(c).at[bsz].set(c_ctx)
    mods = _adaln(cond, ada_w, ada_b).reshape(depth, COND_ROWS, N_MOD, 1, d)

    x_lat = x.reshape(bsz * seq_len, d)
    x_ctx = ctx.reshape(bsz * ctx_len, d)
    lat_rot = _RowMap(seq_len, lambda tm: (lambda i: i // (seq_len // tm)))
    ctx_rot = _RowMap(bsz * ctx_len, lambda tm: (lambda i: bsz))
    row = lambda v: v.reshape(1, -1)

    conv_in_wb = _to_bf16(conv_in_w)[0]

    def layer0(h, rot_fn, sub_len, final_next_mod, weights_bf16):
        a = _modulate(h, mods, 0, rot_fn, _pick(512, rot_fn.tile_limit))
        if weights_bf16 is None:
            gb, u, weights_bf16 = _conv_in(a, conv_in_wb, rot_fn.tile_limit,
                                           (mlp_w1, mlp_w2, conv_out_w, attn_out_w))
        else:
            gb, u, _ = _conv_in(a, conv_in_wb, rot_fn.tile_limit)
        w1b, w2b, conv_out_wb, _ = weights_bf16
        h1, a1 = _conv_out(gb, u, conv_w[0], row(conv_b[0]), conv_out_wb[0], h, mods, 0, rot_fn,
                           row(ln_g[0, 0]), row(ln_b[0, 0]), (0, 3), alpha, sub_len)
        h2, a2 = _mlp(a1, w1b, w2b, h1, mods, 0, rot_fn, row(ln_g[0, 1]), row(ln_b[0, 1]),
                      final_next_mod, alpha)
        return h2, a2, weights_bf16

    h_lat, a_lat, weights_bf16 = layer0(x_lat, lat_rot, seq_len, (1, 0), None)
    _, a_ctx, _ = layer0(x_ctx, ctx_rot, ctx_len, (1, 0), weights_bf16)
    w1b, w2b, _, attn_out_wb = weights_bf16

    cos, sin_signed = _rope_tables(seq_len)
    qg, kg = row(attn_q_gain[0]), row(attn_k_gain[0])
    qkv_lat = _qkv(a_lat, attn_qkv_w[0], qg, kg, (cos, sin_signed), 0, q_dim, kv_dim, seq_len)
    kv_ctx = _qkv(a_ctx, attn_qkv_w[0], qg, kg, None, q_dim, q_dim, kv_dim, ctx_len)
    o = _attention(qkv_lat, kv_ctx, bsz, seq_len, ctx_len, q_dim, kv_dim)
    h_lat, a_lat = _plain_out(o, attn_out_wb[0], h_lat, mods, 1, lat_rot, row(ln_g[1, 0]), row(ln_b[1, 0]),
                              (1, 3), alpha)
    h_lat, _ = _mlp(a_lat, w1b, w2b, h_lat, mods, 1, lat_rot, row(ln_g[1, 1]), row(ln_b[1, 1]),
                    None, alpha)
    return h_lat.reshape(bsz, seq_len, d)
```

```python
import functools
from typing import Any, Callable, NamedTuple

import jax
import jax.numpy as jnp
from jax import lax
from jax.experimental import pallas as pl
from jax.experimental.pallas import tpu as pltpu

F32 = jnp.float32
BF16 = jnp.bfloat16

HEAD_DIM = 128
GQA_GROUP = 4
GRID_W = 64
N_MOD = 6
ROPE_THETA = 10000.0
LN_EPS = 1e-5
RMS_EPS = 1e-6
LOG2_E = 1.4426950408889634
COND_ROWS = 8
BF16_SUBLANES = 16
LANES = 128
VMEM_LIMIT = 60 * 1024 * 1024


class _RowMap(NamedTuple):
    tile_limit: int
    mod_row: Callable


def _cparams(sem):
    return pltpu.CompilerParams(dimension_semantics=sem, vmem_limit_bytes=VMEM_LIMIT)


def _pick(pref, n):
    t = min(pref, n)
    while n % t:
        t //= 2
    return t


def _dot(a, b):
    return jnp.dot(a, b, preferred_element_type=F32)


ADALN_COL_TILE = 1024


def _adaln_kernel(cond_ref, w_ref, b_ref, o_ref):
    c = cond_ref[...]
    s = (c * jax.nn.sigmoid(c)).astype(BF16)
    o_ref[...] = _dot(s, w_ref[...].astype(BF16)) + b_ref[...]


def _adaln(cond, ada_w, ada_b):
    depth, d, n = ada_w.shape
    tn = _pick(ADALN_COL_TILE, n)
    return pl.pallas_call(
        _adaln_kernel,
        out_shape=jax.ShapeDtypeStruct((depth, COND_ROWS, n), F32),
        grid=(depth, n // tn),
        in_specs=[
            pl.BlockSpec((COND_ROWS, d), lambda l, j: (0, 0)),
            pl.BlockSpec((None, d, tn), lambda l, j: (l, 0, j)),
            pl.BlockSpec((None, 1, tn), lambda l, j: (l, 0, j)),
        ],
        out_specs=pl.BlockSpec((None, COND_ROWS, tn), lambda l, j: (l, 0, j)),
        compiler_params=_cparams(("arbitrary", "arbitrary")),
        name="adaln",
    )(cond, ada_w, ada_b.reshape(depth, 1, n))


def _mod_spec(layer, which, row_of_tile, d):
    return pl.BlockSpec((None, None, None, 1, d),
                        lambda i, *_: (layer, row_of_tile(i), which, 0, 0))


def _row_spec(d):
    return pl.BlockSpec((1, d), lambda *_: (0, 0))


CAST_BLOCK_ELEMS = 2 * 1024 * 1024


def _cast_kernel(x_ref, o_ref):
    o_ref[...] = x_ref[...].astype(BF16)


def _to_bf16(w):
    cols = w.shape[-1]
    w2 = w.reshape(-1, cols)
    rows = w2.shape[0]
    want = max(CAST_BLOCK_ELEMS // cols, BF16_SUBLANES)
    tr = _pick(1 << (want.bit_length() - 1), rows)
    out = pl.pallas_call(
        _cast_kernel,
        out_shape=jax.ShapeDtypeStruct((rows, cols), BF16),
        grid=(rows // tr,),
        in_specs=[pl.BlockSpec((tr, cols), lambda i: (i, 0))],
        out_specs=pl.BlockSpec((tr, cols), lambda i: (i, 0)),
        compiler_params=_cparams(("arbitrary",)),
        name="cast_bf16",
    )(w2)
    return out.reshape(w.shape)


MODULATE_ROW_TILE = 512


def _modulate_kernel(x_ref, shift_ref, scale_ref, o_ref):
    o_ref[...] = (x_ref[...] * (1.0 + scale_ref[...]) + shift_ref[...]).astype(BF16)


def _modulate(x, mods, layer, row_of_tile_fn, tm):
    rows, d = x.shape
    rot = row_of_tile_fn.mod_row(tm)
    return pl.pallas_call(
        _modulate_kernel,
        out_shape=jax.ShapeDtypeStruct((rows, d), BF16),
        grid=(rows // tm,),
        in_specs=[
            pl.BlockSpec((tm, d), lambda i: (i, 0)),
            _mod_spec(layer, 0, rot, d),
            _mod_spec(layer, 1, rot, d),
        ],
        out_specs=pl.BlockSpec((tm, d), lambda i: (i, 0)),
        compiler_params=_cparams(("arbitrary",)),
        name="modulate",
    )(x, mods, mods)


CONV_IN_ROW_TILE = 1024
CONV_IN_COL_TILE = 256


def _conv_in_kernel(*refs, n_jobs):
    a_ref, wb_ref, wc_ref, wv_ref = refs[:4]
    job_in = refs[4:4 + n_jobs]
    gb_ref, u_ref = refs[4 + n_jobs:6 + n_jobs]
    job_out = refs[6 + n_jobs:]
    a = a_ref[...]
    gb_ref[...] = _dot(a, wb_ref[...]).astype(BF16)
    u_ref[...] = (_dot(a, wc_ref[...]) * _dot(a, wv_ref[...])).astype(BF16)
    for x_ref, o_ref in zip(job_in, job_out):
        o_ref[...] = x_ref[...].astype(BF16)


def _conv_in(a, w_in, tile_limit, cast_jobs=()):
    rows, d = a.shape
    tm = _pick(CONV_IN_ROW_TILE, tile_limit)
    tn = _pick(CONV_IN_COL_TILE, d)
    nj, ni = d // tn, rows // tm
    jobs2d = [w.reshape(-1, w.shape[-1]) for w in cast_jobs]
    for w2 in jobs2d:
        assert w2.shape[0] % (nj * ni * BF16_SUBLANES) == 0, (w2.shape, nj, ni)
    job_specs = [pl.BlockSpec((w2.shape[0] // (nj * ni), w2.shape[1]), lambda j, i: (j * ni + i, 0)) for w2 in jobs2d]
    out = jax.ShapeDtypeStruct((rows, d), BF16)
    tile_spec = pl.BlockSpec((tm, tn), lambda j, i: (i, j))
    w_spec = lambda t: pl.BlockSpec((d, tn), lambda j, i: (0, j + t * nj))
    gb, u, *casted = pl.pallas_call(
        functools.partial(_conv_in_kernel, n_jobs=len(jobs2d)),
        out_shape=(out, out, *(jax.ShapeDtypeStruct(w2.shape, BF16) for w2 in jobs2d)),
        grid=(nj, ni),
        in_specs=[pl.BlockSpec((tm, d), lambda j, i: (i, 0)), w_spec(0), w_spec(1), w_spec(2), *job_specs],
        out_specs=(tile_spec, tile_spec, *job_specs),
        compiler_params=_cparams(("arbitrary", "arbitrary")),
        name="conv_in",
    )(a, w_in, w_in, w_in, *jobs2d)
    return gb, u, [c.reshape(w.shape) for c, w in zip(casted, cast_jobs)]


LN_TILE_ROWS = 1024
LN_ROW_CHUNK = 64
OUT_SLOTS = 2
ACC_COL_CHUNK = 1024
MLP_FF_TILE = 512
ATTN_OUT_K_TILE = 1024
CONV_OUT_K_TILE = 512


class _LnRefs(NamedTuple):
    hres: Any
    gate: Any
    g: Any
    b: Any
    nshift: Any
    nscale: Any
    h_hbm: Any
    an_hbm: Any
    acc: Any
    ybuf: Any
    ysem: Any
    anbuf: Any
    ansem: Any


def _split_refs(refs, has_next):
    if has_next:
        *head, hres, gate, g, b, nshift, nscale, h_hbm, an_hbm, acc, ybuf, ysem, anbuf, ansem = refs
        return head, _LnRefs(hres, gate, g, b, nshift, nscale, h_hbm, an_hbm, acc, ybuf, ysem, anbuf, ansem)
    *head, hres, gate, g, b, h_hbm, acc, ybuf, ysem = refs
    return head, _LnRefs(hres, gate, g, b, None, None, h_hbm, None, acc, ybuf, ysem, None, None)


def _zero_acc_at_first_step(ln, step):
    @pl.when(step == 0)
    def _():
        ln.acc[...] = jnp.zeros_like(ln.acc)


def _accumulate(ln, lhs, w_ref, alpha, step, n_steps):
    acc = ln.acc
    tm, d = acc.shape
    cw = _pick(ACC_COL_CHUNK, d)
    for n in range(d // cw):
        sl = slice(n * cw, (n + 1) * cw)
        acc[:, sl] += ln.gate[:, sl] * _dot(lhs, w_ref[:, sl])
    rc = tm // n_steps
    r0 = pl.multiple_of(step * rc, rc)
    acc[pl.ds(r0, rc), :] += alpha * ln.hres[...]


def _layer_norm_chunk(ln, slot, rows):
    d = ln.acc.shape[1]
    groups = [slice(t * LANES, (t + 1) * LANES) for t in range(d // LANES)]
    total = None
    for g in groups:
        total = ln.acc[rows, g] if total is None else total + ln.acc[rows, g]
    mu = jnp.sum(total, axis=-1, keepdims=True) / d
    total = None
    for g in groups:
        zc = ln.acc[rows, g] - mu
        total = zc * zc if total is None else total + zc * zc
    rstd = lax.rsqrt(jnp.sum(total, axis=-1, keepdims=True) / d + LN_EPS)
    for g in groups:
        y = (ln.acc[rows, g] - mu) * rstd * ln.g[:, g] + ln.b[:, g]
        ln.ybuf[slot, :, g] = y
        if ln.an_hbm is not None:
            ln.anbuf[slot, :, g] = (y * (1.0 + ln.nscale[:, g]) + ln.nshift[:, g]).astype(BF16)


def _layer_norm_rows(ln, row0):
    tm, d = ln.acc.shape
    rb = ln.ybuf.shape[1]

    def copies(slot, r0):
        dst = pl.ds(row0 + r0, rb)
        cs = [pltpu.make_async_copy(ln.ybuf.at[slot], ln.h_hbm.at[dst], ln.ysem.at[slot])]
        if ln.an_hbm is not None:
            cs.append(pltpu.make_async_copy(ln.anbuf.at[slot], ln.an_hbm.at[dst], ln.ansem.at[slot]))
        return cs

    def body(turn, carry):
        for slot in range(OUT_SLOTS):
            r0 = pl.multiple_of((turn * OUT_SLOTS + slot) * rb, rb)

            @pl.when(turn > 0)
            def _():
                for c in copies(slot, r0):
                    c.wait()

            _layer_norm_chunk(ln, slot, pl.ds(r0, rb))
            for c in copies(slot, r0):
                c.start()
        return carry

    lax.fori_loop(0, tm // (OUT_SLOTS * rb), body, 0)
    for slot in range(OUT_SLOTS):
        for c in copies(slot, 0):
            c.wait()


def _finish_rows(ln, step, n_steps):
    row0 = pl.program_id(0) * ln.acc.shape[0]

    @pl.when(step == n_steps - 1)
    def _():
        _layer_norm_rows(ln, row0)


def _mlp_kernel(*refs, alpha, n_steps, has_next):
    (a_ref, w1_ref, w2_ref), ln = _split_refs(refs, has_next)
    step = pl.program_id(1)
    _zero_acc_at_first_step(ln, step)
    hid = jnp.maximum(_dot(a_ref[...], w1_ref[...]), 0.0)
    _accumulate(ln, (hid * hid).astype(BF16), w2_ref, alpha, step, n_steps)
    _finish_rows(ln, step, n_steps)


def _plain_out_kernel(*refs, alpha, n_steps, has_next):
    (lhs_ref, w_ref), ln = _split_refs(refs, has_next)
    step = pl.program_id(1)
    _zero_acc_at_first_step(ln, step)
    _accumulate(ln, lhs_ref[...], w_ref, alpha, step, n_steps)
    _finish_rows(ln, step, n_steps)


def _conv_out_kernel(*refs, alpha, n_steps, has_next, seq_len):
    (gb_ref, u_ref, up_ref, un_ref, cw_ref, cb_ref, w_ref), ln = _split_refs(refs, has_next)
    step = pl.program_id(1)
    _zero_acc_at_first_step(ln, step)
    tm = u_ref.shape[0]
    u = u_ref[...].astype(F32)
    local = lax.broadcasted_iota(jnp.int32, (tm, 1), 0)
    pos = (local + pl.program_id(0) * tm) % seq_len
    prev_row = up_ref[BF16_SUBLANES - 1:BF16_SUBLANES, :].astype(F32)
    next_row = un_ref[0:1, :].astype(F32)
    u_prev = jnp.where(local == 0, prev_row, pltpu.roll(u, 1, axis=0))
    u_prev = jnp.where(pos == 0, 0.0, u_prev)
    u_next = jnp.where(local == tm - 1, next_row, pltpu.roll(u, tm - 1, axis=0))
    u_next = jnp.where(pos == seq_len - 1, 0.0, u_next)
    conv = u_prev * cw_ref[0:1, :] + u * cw_ref[1:2, :] + u_next * cw_ref[2:3, :] + cb_ref[...]
    lhs = (gb_ref[...].astype(F32) * conv).astype(BF16)
    _accumulate(ln, lhs, w_ref, alpha, step, n_steps)
    _finish_rows(ln, step, n_steps)


def _residual_ln_call(kernel, name, lead_specs, lead_args, hres, mods, layer, gate_idx, rot_fn,
                      ln_g, ln_b, next_mod, tm, n_steps, alpha, **kernel_kwargs):
    rows, d = hres.shape
    rc = tm // n_steps
    rb = _pick(LN_ROW_CHUNK, tm // OUT_SLOTS)
    assert rc * n_steps == tm and rc % 8 == 0 and tm % (OUT_SLOTS * rb) == 0, (tm, n_steps, rb)
    rot = rot_fn.mod_row(tm)
    in_specs = list(lead_specs) + [
        pl.BlockSpec((rc, d), lambda i, k: (i * n_steps + k, 0)),
        _mod_spec(layer, gate_idx, rot, d),
        _row_spec(d),
        _row_spec(d),
    ]
    args = list(lead_args) + [hres, mods, ln_g, ln_b]
    out_shape = [jax.ShapeDtypeStruct((rows, d), F32)]
    scratch = [pltpu.VMEM((tm, d), F32), pltpu.VMEM((OUT_SLOTS, rb, d), F32), pltpu.SemaphoreType.DMA((OUT_SLOTS,))]
    has_next = next_mod is not None
    if has_next:
        nl, nidx = next_mod
        in_specs += [_mod_spec(nl, nidx, rot, d), _mod_spec(nl, nidx + 1, rot, d)]
        args += [mods, mods]
        out_shape.append(jax.ShapeDtypeStruct((rows, d), BF16))
        scratch += [pltpu.VMEM((OUT_SLOTS, rb, d), BF16), pltpu.SemaphoreType.DMA((OUT_SLOTS,))]
    res = pl.pallas_call(
        functools.partial(kernel, alpha=alpha, n_steps=n_steps, has_next=has_next, **kernel_kwargs),
        out_shape=tuple(out_shape),
        grid=(rows // tm, n_steps),
        in_specs=in_specs,
        out_specs=tuple(pl.BlockSpec(memory_space=pl.ANY) for _ in out_shape),
        scratch_shapes=scratch,
        compiler_params=_cparams(("arbitrary", "arbitrary")),
        name=name,
    )(*args)
    return res if has_next else (res[0], None)


def _mlp(a, w1, w2, hres, mods, layer, rot_fn, ln_g, ln_b, next_mod, alpha):
    rows, d = a.shape
    d_ff = w1.shape[2]
    tm = _pick(LN_TILE_ROWS, rot_fn.tile_limit)
    tf = _pick(MLP_FF_TILE, d_ff)
    lead_specs = [
        pl.BlockSpec((tm, d), lambda i, k: (i, 0)),
        pl.BlockSpec((None, d, tf), lambda i, k: (layer, 0, k)),
        pl.BlockSpec((None, tf, d), lambda i, k: (layer, k, 0)),
    ]
    return _residual_ln_call(_mlp_kernel, "mlp", lead_specs, [a, w1, w2], hres, mods, layer, 5, rot_fn,
                             ln_g, ln_b, next_mod, tm, d_ff // tf, alpha)


def _plain_out(lhs, w, hres, mods, layer, rot_fn, ln_g, ln_b, next_mod, alpha):
    rows, kdim = lhs.shape
    d = w.shape[1]
    tm = _pick(LN_TILE_ROWS, rot_fn.tile_limit)
    tk = _pick(ATTN_OUT_K_TILE, kdim)
    lead_specs = [
        pl.BlockSpec((tm, tk), lambda i, k: (i, k)),
        pl.BlockSpec((tk, d), lambda i, k: (k, 0)),
    ]
    return _residual_ln_call(_plain_out_kernel, "attn_out", lead_specs, [lhs, w], hres, mods, layer, 2,
                             rot_fn, ln_g, ln_b, next_mod, tm, kdim // tk, alpha)


def _conv_out(gb, u, conv_w, conv_b, w, hres, mods, layer, rot_fn, ln_g, ln_b, next_mod, alpha, seq_len):
    rows, kdim = u.shape
    d = w.shape[1]
    tm = _pick(LN_TILE_ROWS, rot_fn.tile_limit)
    tk = _pick(CONV_OUT_K_TILE, kdim)
    hb = BF16_SUBLANES
    n_halo = rows // hb
    lead_specs = [
        pl.BlockSpec((tm, tk), lambda i, k: (i, k)),
        pl.BlockSpec((tm, tk), lambda i, k: (i, k)),
        pl.BlockSpec((hb, tk), lambda i, k: (jnp.maximum(i * (tm // hb) - 1, 0), k)),
        pl.BlockSpec((hb, tk), lambda i, k: (jnp.minimum((i + 1) * (tm // hb), n_halo - 1), k)),
        pl.BlockSpec((3, tk), lambda i, k: (0, k)),
        pl.BlockSpec((1, tk), lambda i, k: (0, k)),
        pl.BlockSpec((tk, d), lambda i, k: (k, 0)),
    ]
    return _residual_ln_call(_conv_out_kernel, "conv_out", lead_specs, [gb, u, u, u, conv_w, conv_b, w],
                             hres, mods, layer, 2, rot_fn, ln_g, ln_b, next_mod, tm, kdim // tk, alpha,
                             seq_len=seq_len)


QKV_ROW_TILE = 1024
QKV_COL_TILE = 512


def _rms_rope_heads(y, gain_ref, rope_refs, o_ref):
    tm, tn = y.shape
    if rope_refs is not None:
        cos_ref, sin_ref = rope_refs
        lane = lax.broadcasted_iota(jnp.int32, (tm, HEAD_DIM), 1)
        first_half = (lane % (HEAD_DIM // 2)) < (HEAD_DIM // 4)
    for h in range(tn // HEAD_DIM):
        sl = slice(h * HEAD_DIM, (h + 1) * HEAD_DIM)
        x = y[:, sl]
        x = x * lax.rsqrt(jnp.mean(x * x, axis=-1, keepdims=True) + RMS_EPS) * gain_ref[...]
        if rope_refs is not None:
            rot = jnp.where(first_half,
                            pltpu.roll(x, HEAD_DIM - HEAD_DIM // 4, axis=1),
                            pltpu.roll(x, HEAD_DIM // 4, axis=1))
            x = x * cos_ref[...] + rot * sin_ref[...]
        o_ref[:, sl] = x.astype(BF16)


def _qkv_kernel(*refs, n_q_tiles, n_k_tiles, rope):
    if rope:
        a_ref, w_ref, qg_ref, kg_ref, cos_ref, sin_ref, o_ref, ws, y0_ref, y1_ref = refs
        rope_refs = (cos_ref, sin_ref)
    else:
        a_ref, w_ref, qg_ref, kg_ref, o_ref, ws, y0_ref, y1_ref = refs
        rope_refs = None
    j, i = pl.program_id(0), pl.program_id(1)

    @pl.when(i == 0)
    def _():
        ws[...] = w_ref[...].astype(BF16)

    @pl.when((i == 0) & (j == 0))
    def _():
        y1_ref[...] = jnp.zeros_like(y1_ref)

    def plain(y):
        o_ref[...] = y.astype(BF16)

    kinds = [((j >= n_q_tiles) & (j < n_q_tiles + n_k_tiles), lambda y: _rms_rope_heads(y, kg_ref, rope_refs, o_ref)),
             (j >= n_q_tiles + n_k_tiles, plain)]
    if n_q_tiles:
        kinds.append((j < n_q_tiles, lambda y: _rms_rope_heads(y, qg_ref, rope_refs, o_ref)))
    for parity, (y_new, y_old) in enumerate([(y0_ref, y1_ref), (y1_ref, y0_ref)]):
        for is_kind, epilogue in kinds:
            @pl.when((i % 2 == parity) & is_kind)
            def _(y_new=y_new, y_old=y_old, epilogue=epilogue):
                y_new[...] = _dot(a_ref[...], ws[...])
                epilogue(y_old[...])


def _qkv(a, w_qkv, q_gain, k_gain, rope_tables, col0, q_dim, kv_dim, seq_len):
    rows, d = a.shape
    n_cols = w_qkv.shape[1] - col0
    tm = _pick(QKV_ROW_TILE, seq_len if rope_tables is not None else rows)
    tn = _pick(QKV_COL_TILE, kv_dim)
    assert col0 % tn == 0 and q_dim % tn == 0
    j0 = col0 // tn
    n_q_tiles = (q_dim - col0) // tn if col0 < q_dim else 0
    n_k_tiles = kv_dim // tn
    rope = rope_tables is not None
    n_row_tiles = rows // tm
    in_specs = [
        pl.BlockSpec((tm, d), lambda j, i: (jnp.minimum(i, n_row_tiles - 1), 0)),
        pl.BlockSpec((d, tn), lambda j, i: (0, j + j0)),
        pl.BlockSpec((1, HEAD_DIM), lambda j, i: (0, 0)),
        pl.BlockSpec((1, HEAD_DIM), lambda j, i: (0, 0)),
    ]
    args = [a, w_qkv, q_gain, k_gain]
    if rope:
        tiles_per_seq = seq_len // tm
        tab = pl.BlockSpec((tm, HEAD_DIM), lambda j, i: (jnp.maximum(i - 1, 0) % tiles_per_seq, 0))
        in_specs += [tab, tab]
        args += list(rope_tables)
    return pl.pallas_call(
        functools.partial(_qkv_kernel, n_q_tiles=n_q_tiles, n_k_tiles=n_k_tiles, rope=rope),
        out_shape=jax.ShapeDtypeStruct((rows, n_cols), BF16),
        grid=(n_cols // tn, n_row_tiles + 1),
        in_specs=in_specs,
        out_specs=pl.BlockSpec((tm, tn), lambda j, i: (jnp.maximum(i - 1, 0), j)),
        scratch_shapes=[pltpu.VMEM((d, tn), BF16), pltpu.VMEM((tm, tn), F32), pltpu.VMEM((tm, tn), F32)],
        compiler_params=_cparams(("arbitrary", "arbitrary")),
        name="qkv" if rope else "kv_ctx",
    )(*args)


def _rope_tables(seq_len):
    rows = seq_len // GRID_W
    row = jnp.repeat(jnp.arange(rows), GRID_W)
    col = jnp.tile(jnp.arange(GRID_W), rows)
    half = HEAD_DIM // 2
    inv_freq = ROPE_THETA ** (-jnp.arange(0, half, 2, dtype=F32) / half)

    def axis_angles(pos):
        ang = pos.astype(F32)[:, None] * inv_freq[None, :]
        return jnp.concatenate([ang, ang], axis=-1)

    ang = jnp.concatenate([axis_angles(row), axis_angles(col)], axis=-1)
    lane = jnp.arange(HEAD_DIM)
    sign = jnp.where((lane % half) < (HEAD_DIM // 4), -1.0, 1.0).astype(F32)
    return jnp.cos(ang), jnp.sin(ang) * sign[None, :]


_NT = (((1,), (1,)), ((), ()))
ATTN_KEY_CHUNK = 256
ATTN_Q_TILE = 256


_TN = (((0,), (0,)), ((), ()))
SUBLANES = 8
_DONE = object()


def _row_group_fold(x, op):
    n = x.shape[0] // SUBLANES
    return op(x.reshape(n, SUBLANES, x.shape[1]), axis=0)


def _score_tile(q_ref, kc_ref, kl_ref, s_ref, m_ref, key_chunk):
    ctx_len, seq_len = kc_ref.shape[0], kl_ref.shape[0]
    q = jnp.concatenate([q_ref[:, g * HEAD_DIM:(g + 1) * HEAD_DIM] for g in range(GQA_GROUP)], axis=0)
    segs = [(kc_ref, 0, ctx_len, 0)] + [(kl_ref, r, key_chunk, ctx_len + r) for r in range(0, seq_len, key_chunk)]
    m_rows = None
    for k_ref, r0, cnt, c0 in segs:
        st = lax.dot_general(k_ref[r0:r0 + cnt, :], q, _NT, preferred_element_type=F32)
        s_ref[c0:c0 + cnt, :] = st
        fold = _row_group_fold(st, jnp.max)
        m_rows = fold if m_rows is None else jnp.maximum(m_rows, fold)
        yield
    m_ref[...] = jnp.broadcast_to(jnp.max(m_rows, axis=0, keepdims=True), m_ref.shape)


def _finish_tile(s_ref, m_ref, vc_ref, vl_ref, o_ref, row0, scale, key_chunk):
    ctx_len, seq_len = vc_ref.shape[0], vl_ref.shape[0]
    rows = s_ref.shape[1]
    tq = rows // GQA_GROUP
    segs = [(vc_ref, 0, ctx_len, 0)] + [(vl_ref, r, key_chunk, ctx_len + r) for r in range(0, seq_len, key_chunk)]
    m = m_ref[0:1, :]
    l_rows = jnp.zeros((SUBLANES, rows), F32)
    o_t = jnp.zeros((HEAD_DIM, rows), F32)
    for v_ref, r0, cnt, c0 in segs:
        p_t = jnp.exp2((s_ref[c0:c0 + cnt, :] - m) * (scale * LOG2_E))
        l_rows = l_rows + _row_group_fold(p_t, jnp.sum)
        o_t = o_t + lax.dot_general(v_ref[r0:r0 + cnt, :], p_t.astype(BF16), _TN, preferred_element_type=F32)
        yield
    o = (o_t / jnp.sum(l_rows, axis=0, keepdims=True)).T
    for g in range(GQA_GROUP):
        o_ref[row0:row0 + tq, g * HEAD_DIM:(g + 1) * HEAD_DIM] = o[g * tq:(g + 1) * tq, :].astype(BF16)


def _attention_kernel(q_ref, kc_ref, kl_ref, vc_ref, vl_ref, o_ref, s0_ref, s1_ref, m0_ref, m1_ref,
                      *, scale, key_chunk):
    k = pl.program_id(0)

    @pl.when(k == 0)
    def _():
        s1_ref[...] = jnp.zeros_like(s1_ref)
        m1_ref[...] = jnp.zeros_like(m1_ref)

    def step(s_new, m_new, s_old, m_old):
        stages = [_finish_tile(s_old, m_old, vc_ref, vl_ref, o_ref, 0, scale, key_chunk),
                  _score_tile(q_ref, kc_ref, kl_ref, s_new, m_new, key_chunk)]
        while stages:
            stages = [stage for stage in stages if next(stage, _DONE) is not _DONE]

    @pl.when(k % 2 == 0)
    def _():
        step(s0_ref, m0_ref, s1_ref, m1_ref)

    @pl.when(k % 2 == 1)
    def _():
        step(s1_ref, m1_ref, s0_ref, m0_ref)


def _attention(qkv_lat, kv_ctx, bsz, seq_len, ctx_len, q_dim, kv_dim):
    n_kv = kv_dim // HEAD_DIM
    gw = GQA_GROUP * HEAD_DIM
    tq = _pick(ATTN_Q_TILE, seq_len)
    tiles = seq_len // tq
    n_tiles = bsz * n_kv * tiles
    k0 = q_dim // HEAD_DIM
    v0 = k0 + n_kv
    key_chunk = _pick(ATTN_KEY_CHUNK, seq_len)
    assert ctx_len % HEAD_DIM == 0 and key_chunk % HEAD_DIM == 0

    def tile_of(n):
        n = jnp.clip(n, 0, n_tiles - 1)
        return n // (n_kv * tiles), (n // tiles) % n_kv, n % tiles

    def at(offset, f):
        return lambda k: f(*tile_of(k + offset))

    rows = GQA_GROUP * tq
    return pl.pallas_call(
        functools.partial(_attention_kernel, scale=HEAD_DIM ** -0.5, key_chunk=key_chunk),
        out_shape=jax.ShapeDtypeStruct((bsz * seq_len, q_dim), BF16),
        grid=(n_tiles + 1,),
        in_specs=[
            pl.BlockSpec((tq, gw), at(0, lambda b, h, t: (b * tiles + t, h))),
            pl.BlockSpec((ctx_len, HEAD_DIM), at(0, lambda b, h, t: (b, h))),
            pl.BlockSpec((seq_len, HEAD_DIM), at(0, lambda b, h, t: (b, k0 + h))),
            pl.BlockSpec((ctx_len, HEAD_DIM), at(-1, lambda b, h, t: (b, n_kv + h))),
            pl.BlockSpec((seq_len, HEAD_DIM), at(-1, lambda b, h, t: (b, v0 + h))),
        ],
        out_specs=pl.BlockSpec((tq, gw), at(-1, lambda b, h, t: (b * tiles + t, h))),
        scratch_shapes=[pltpu.VMEM((ctx_len + seq_len, rows), F32), pltpu.VMEM((ctx_len + seq_len, rows), F32),
                        pltpu.VMEM((SUBLANES, rows), F32), pltpu.VMEM((SUBLANES, rows), F32)],
        compiler_params=_cparams(("arbitrary",)),
        name="attention",
    )(qkv_lat, kv_ctx, qkv_lat, kv_ctx, qkv_lat)


def kernel(x, c, ctx, c_ctx, ada_w, ada_b, ln_g, ln_b, mlp_w1, mlp_w2, conv_in_w, conv_w, conv_b, conv_out_w,
           attn_qkv_w, attn_q_gain, attn_k_gain, attn_out_w):
    bsz, seq_len, d = x.shape
    ctx_len = ctx.shape[1]
    depth = ada_w.shape[0]
    assert depth == 2 and bsz < COND_ROWS, "one conv-mixer layer followed by one attention layer"
    alpha = (2 * depth) ** 0.25
    kv_dim = (attn_qkv_w.shape[2] - d) // 2
    q_dim = d

    cond = jnp.zeros((COND_ROWS, d), F32).at[:bsz].set(c).at[bsz].set(c_ctx)
    mods = _adaln(cond, ada_w, ada_b).reshape(depth, COND_ROWS, N_MOD, 1, d)

    x_lat = x.reshape(bsz * seq_len, d)
    x_ctx = ctx.reshape(bsz * ctx_len, d)
    lat_rot = _RowMap(seq_len, lambda tm: (lambda i: i // (seq_len // tm)))
    ctx_rot = _RowMap(bsz * ctx_len, lambda tm: (lambda i: bsz))
    row = lambda v: v.reshape(1, -1)

    conv_in_wb = _to_bf16(conv_in_w)[0]

    def layer0(h, rot_fn, sub_len, final_next_mod, weights_bf16):
        a = _modulate(h, mods, 0, rot_fn, _pick(MODULATE_ROW_TILE, rot_fn.tile_limit))
        if weights_bf16 is None:
            gb, u, weights_bf16 = _conv_in(a, conv_in_wb, rot_fn.tile_limit,
                                           (mlp_w1, mlp_w2, conv_out_w, attn_out_w))
        else:
            gb, u, _ = _conv_in(a, conv_in_wb, rot_fn.tile_limit)
        w1b, w2b, conv_out_wb, _ = weights_bf16
        h1, a1 = _conv_out(gb, u, conv_w[0], row(conv_b[0]), conv_out_wb[0], h, mods, 0, rot_fn,
                           row(ln_g[0, 0]), row(ln_b[0, 0]), (0, 3), alpha, sub_len)
        h2, a2 = _mlp(a1, w1b, w2b, h1, mods, 0, rot_fn, row(ln_g[0, 1]), row(ln_b[0, 1]),
                      final_next_mod, alpha)
        return h2, a2, weights_bf16

    h_lat, a_lat, weights_bf16 = layer0(x_lat, lat_rot, seq_len, (1, 0), None)
    _, a_ctx, _ = layer0(x_ctx, ctx_rot, ctx_len, (1, 0), weights_bf16)
    w1b, w2b, _, attn_out_wb = weights_bf16

    cos, sin_signed = _rope_tables(seq_len)
    qg, kg = row(attn_q_gain[0]), row(attn_k_gain[0])
    qkv_lat = _qkv(a_lat, attn_qkv_w[0], qg, kg, (cos, sin_signed), 0, q_dim, kv_dim, seq_len)
    kv_ctx = _qkv(a_ctx, attn_qkv_w[0], qg, kg, None, q_dim, q_dim, kv_dim, ctx_len)
    o = _attention(qkv_lat, kv_ctx, bsz, seq_len, ctx_len, q_dim, kv_dim)
    h_lat, a_lat = _plain_out(o, attn_out_wb[0], h_lat, mods, 1, lat_rot, row(ln_g[1, 0]), row(ln_b[1, 0]),
                              (1, 3), alpha)
    h_lat, _ = _mlp(a_lat, w1b, w2b, h_lat, mods, 1, lat_rot, row(ln_g[1, 1]), row(ln_b[1, 1]),
                    None, alpha)
    return h_lat.reshape(bsz, seq_len, d)
```

```python
import functools
from typing import Any, Callable, NamedTuple

import jax
import jax.numpy as jnp
from jax import lax
from jax.experimental import pallas as pl
from jax.experimental.pallas import tpu as pltpu

F32 = jnp.float32
BF16 = jnp.bfloat16

HEAD_DIM = 128
GQA_GROUP = 4
GRID_W = 64
N_MOD = 6
ROPE_THETA = 10000.0
LN_EPS = 1e-5
RMS_EPS = 1e-6
LOG2_E = 1.4426950408889634
COND_ROWS = 8
BF16_SUBLANES = 16
LANES = 128
VMEM_LIMIT = 60 * 1024 * 1024


class _RowMap(NamedTuple):
    tile_limit: int
    mod_row: Callable


def _cparams(sem):
    return pltpu.CompilerParams(dimension_semantics=sem, vmem_limit_bytes=VMEM_LIMIT)


def _pick(pref, n):
    t = min(pref, n)
    while n % t:
        t //= 2
    return t


def _dot(a, b):
    return jnp.dot(a, b, preferred_element_type=F32)


ADALN_COL_TILE = 1024


def _adaln_kernel(cond_ref, w_ref, b_ref, o_ref):
    c = cond_ref[...]
    s = (c * jax.nn.sigmoid(c)).astype(BF16)
    o_ref[...] = _dot(s, w_ref[...].astype(BF16)) + b_ref[...]


def _adaln(cond, ada_w, ada_b):
    depth, d, n = ada_w.shape
    tn = _pick(ADALN_COL_TILE, n)
    return pl.pallas_call(
        _adaln_kernel,
        out_shape=jax.ShapeDtypeStruct((depth, COND_ROWS, n), F32),
        grid=(depth, n // tn),
        in_specs=[
            pl.BlockSpec((COND_ROWS, d), lambda l, j: (0, 0)),
            pl.BlockSpec((None, d, tn), lambda l, j: (l, 0, j)),
            pl.BlockSpec((None, 1, tn), lambda l, j: (l, 0, j)),
        ],
        out_specs=pl.BlockSpec((None, COND_ROWS, tn), lambda l, j: (l, 0, j)),
        compiler_params=_cparams(("arbitrary", "arbitrary")),
        name="adaln",
    )(cond, ada_w, ada_b.reshape(depth, 1, n))


def _mod_spec(layer, which, row_of_tile, d):
    return pl.BlockSpec((None, None, None, 1, d),
                        lambda i, *_: (layer, row_of_tile(i), which, 0, 0))


def _row_spec(d):
    return pl.BlockSpec((1, d), lambda *_: (0, 0))


CAST_BLOCK_ELEMS = 2 * 1024 * 1024


def _cast_kernel(x_ref, o_ref):
    o_ref[...] = x_ref[...].astype(BF16)


def _to_bf16(w):
    cols = w.shape[-1]
    w2 = w.reshape(-1, cols)
    rows = w2.shape[0]
    want = max(CAST_BLOCK_ELEMS // cols, BF16_SUBLANES)
    tr = _pick(1 << (want.bit_length() - 1), rows)
    out = pl.pallas_call(
        _cast_kernel,
        out_shape=jax.ShapeDtypeStruct((rows, cols), BF16),
        grid=(rows // tr,),
        in_specs=[pl.BlockSpec((tr, cols), lambda i: (i, 0))],
        out_specs=pl.BlockSpec((tr, cols), lambda i: (i, 0)),
        compiler_params=_cparams(("arbitrary",)),
        name="cast_bf16",
    )(w2)
    return out.reshape(w.shape)


MODULATE_ROW_TILE = 512


def _modulate_kernel(x_ref, shift_ref, scale_ref, o_ref):
    o_ref[...] = (x_ref[...] * (1.0 + scale_ref[...]) + shift_ref[...]).astype(BF16)


def _modulate(x, mods, layer, row_of_tile_fn, tm):
    rows, d = x.shape
    rot = row_of_tile_fn.mod_row(tm)
    return pl.pallas_call(
        _modulate_kernel,
        out_shape=jax.ShapeDtypeStruct((rows, d), BF16),
        grid=(rows // tm,),
        in_specs=[
            pl.BlockSpec((tm, d), lambda i: (i, 0)),
            _mod_spec(layer, 0, rot, d),
            _mod_spec(layer, 1, rot, d),
        ],
        out_specs=pl.BlockSpec((tm, d), lambda i: (i, 0)),
        compiler_params=_cparams(("arbitrary",)),
        name="modulate",
    )(x, mods, mods)


CONV_IN_ROW_TILE = 1024
CONV_IN_COL_TILE = 256


def _conv_in_kernel(*refs, n_jobs):
    a_ref, wb_ref, wc_ref, wv_ref = refs[:4]
    job_in = refs[4:4 + n_jobs]
    gb_ref, u_ref = refs[4 + n_jobs:6 + n_jobs]
    job_out = refs[6 + n_jobs:]
    a = a_ref[...]
    gb_ref[...] = _dot(a, wb_ref[...]).astype(BF16)
    u_ref[...] = (_dot(a, wc_ref[...]) * _dot(a, wv_ref[...])).astype(BF16)
    for x_ref, o_ref in zip(job_in, job_out):
        o_ref[...] = x_ref[...].astype(BF16)


def _conv_in(a, w_in, tile_limit, cast_jobs=()):
    rows, d = a.shape
    tm = _pick(CONV_IN_ROW_TILE, tile_limit)
    tn = _pick(CONV_IN_COL_TILE, d)
    nj, ni = d // tn, rows // tm
    jobs2d = [w.reshape(-1, w.shape[-1]) for w in cast_jobs]
    for w2 in jobs2d:
        assert w2.shape[0] % (nj * ni * BF16_SUBLANES) == 0, (w2.shape, nj, ni)
    job_specs = [pl.BlockSpec((w2.shape[0] // (nj * ni), w2.shape[1]), lambda j, i: (j * ni + i, 0)) for w2 in jobs2d]
    out = jax.ShapeDtypeStruct((rows, d), BF16)
    tile_spec = pl.BlockSpec((tm, tn), lambda j, i: (i, j))
    w_spec = lambda t: pl.BlockSpec((d, tn), lambda j, i: (0, j + t * nj))
    gb, u, *casted = pl.pallas_call(
        functools.partial(_conv_in_kernel, n_jobs=len(jobs2d)),
        out_shape=(out, out, *(jax.ShapeDtypeStruct(w2.shape, BF16) for w2 in jobs2d)),
        grid=(nj, ni),
        in_specs=[pl.BlockSpec((tm, d), lambda j, i: (i, 0)), w_spec(0), w_spec(1), w_spec(2), *job_specs],
        out_specs=(tile_spec, tile_spec, *job_specs),
        compiler_params=_cparams(("arbitrary", "arbitrary")),
        name="conv_in",
    )(a, w_in, w_in, w_in, *jobs2d)
    return gb, u, [c.reshape(w.shape) for c, w in zip(casted, cast_jobs)]


LN_TILE_ROWS = 1024
LN_ROW_CHUNK = 64
OUT_SLOTS = 2
ACC_COL_CHUNK = 1024
MLP_FF_TILE = 512
ATTN_OUT_K_TILE = 1024
CONV_OUT_K_TILE = 512


class _LnRefs(NamedTuple):
    hres: Any
    gate: Any
    g: Any
    b: Any
    nshift: Any
    nscale: Any
    h_hbm: Any
    an_hbm: Any
    acc: Any
    ybuf: Any
    ysem: Any
    anbuf: Any
    ansem: Any


def _split_refs(refs, has_next):
    if has_next:
        *head, hres, gate, g, b, nshift, nscale, h_hbm, an_hbm, acc, ybuf, ysem, anbuf, ansem = refs
        return head, _LnRefs(hres, gate, g, b, nshift, nscale, h_hbm, an_hbm, acc, ybuf, ysem, anbuf, ansem)
    *head, hres, gate, g, b, h_hbm, acc, ybuf, ysem = refs
    return head, _LnRefs(hres, gate, g, b, None, None, h_hbm, None, acc, ybuf, ysem, None, None)


def _zero_acc_at_first_step(ln, step):
    @pl.when(step == 0)
    def _():
        ln.acc[...] = jnp.zeros_like(ln.acc)


def _accumulate(ln, lhs, w_ref, alpha, step, n_steps):
    acc = ln.acc
    tm, d = acc.shape
    cw = _pick(ACC_COL_CHUNK, d)
    for n in range(d // cw):
        sl = slice(n * cw, (n + 1) * cw)
        acc[:, sl] += ln.gate[:, sl] * _dot(lhs, w_ref[:, sl])
    rc = tm // n_steps
    r0 = pl.multiple_of(step * rc, rc)
    acc[pl.ds(r0, rc), :] += alpha * ln.hres[...]


def _layer_norm_chunk(ln, slot, rows):
    d = ln.acc.shape[1]
    groups = [slice(t * LANES, (t + 1) * LANES) for t in range(d // LANES)]
    total = None
    for g in groups:
        total = ln.acc[rows, g] if total is None else total + ln.acc[rows, g]
    mu = jnp.sum(total, axis=-1, keepdims=True) / d
    total = None
    for g in groups:
        zc = ln.acc[rows, g] - mu
        total = zc * zc if total is None else total + zc * zc
    rstd = lax.rsqrt(jnp.sum(total, axis=-1, keepdims=True) / d + LN_EPS)
    for g in groups:
        y = (ln.acc[rows, g] - mu) * rstd * ln.g[:, g] + ln.b[:, g]
        ln.ybuf[slot, :, g] = y
        if ln.an_hbm is not None:
            ln.anbuf[slot, :, g] = (y * (1.0 + ln.nscale[:, g]) + ln.nshift[:, g]).astype(BF16)


def _layer_norm_rows(ln, row0):
    tm, d = ln.acc.shape
    rb = ln.ybuf.shape[1]

    def copies(slot, r0):
        dst = pl.ds(row0 + r0, rb)
        cs = [pltpu.make_async_copy(ln.ybuf.at[slot], ln.h_hbm.at[dst], ln.ysem.at[slot])]
        if ln.an_hbm is not None:
            cs.append(pltpu.make_async_copy(ln.anbuf.at[slot], ln.an_hbm.at[dst], ln.ansem.at[slot]))
        return cs

    def body(turn, carry):
        for slot in range(OUT_SLOTS):
            r0 = pl.multiple_of((turn * OUT_SLOTS + slot) * rb, rb)

            @pl.when(turn > 0)
            def _():
                for c in copies(slot, r0):
                    c.wait()

            _layer_norm_chunk(ln, slot, pl.ds(r0, rb))
            for c in copies(slot, r0):
                c.start()
        return carry

    lax.fori_loop(0, tm // (OUT_SLOTS * rb), body, 0)
    for slot in range(OUT_SLOTS):
        for c in copies(slot, 0):
            c.wait()


def _finish_rows(ln, step, n_steps):
    row0 = pl.program_id(0) * ln.acc.shape[0]

    @pl.when(step == n_steps - 1)
    def _():
        _layer_norm_rows(ln, row0)


def _mlp_kernel(*refs, alpha, n_steps, has_next):
    (a_ref, w1_ref, w2_ref), ln = _split_refs(refs, has_next)
    step = pl.program_id(1)
    _zero_acc_at_first_step(ln, step)
    hid = jnp.maximum(_dot(a_ref[...], w1_ref[...]), 0.0)
    _accumulate(ln, (hid * hid).astype(BF16), w2_ref, alpha, step, n_steps)
    _finish_rows(ln, step, n_steps)


def _plain_out_kernel(*refs, alpha, n_steps, has_next):
    (lhs_ref, w_ref), ln = _split_refs(refs, has_next)
    step = pl.program_id(1)
    _zero_acc_at_first_step(ln, step)
    _accumulate(ln, lhs_ref[...], w_ref, alpha, step, n_steps)
    _finish_rows(ln, step, n_steps)


def _conv_out_kernel(*refs, alpha, n_steps, has_next, seq_len):
    (gb_ref, u_ref, up_ref, un_ref, cw_ref, cb_ref, w_ref), ln = _split_refs(refs, has_next)
    step = pl.program_id(1)
    _zero_acc_at_first_step(ln, step)
    tm = u_ref.shape[0]
    u = u_ref[...].astype(F32)
    local = lax.broadcasted_iota(jnp.int32, (tm, 1), 0)
    pos = (local + pl.program_id(0) * tm) % seq_len
    prev_row = up_ref[BF16_SUBLANES - 1:BF16_SUBLANES, :].astype(F32)
    next_row = un_ref[0:1, :].astype(F32)
    u_prev = jnp.where(local == 0, prev_row, pltpu.roll(u, 1, axis=0))
    u_prev = jnp.where(pos == 0, 0.0, u_prev)
    u_next = jnp.where(local == tm - 1, next_row, pltpu.roll(u, tm - 1, axis=0))
    u_next = jnp.where(pos == seq_len - 1, 0.0, u_next)
    conv = u_prev * cw_ref[0:1, :] + u * cw_ref[1:2, :] + u_next * cw_ref[2:3, :] + cb_ref[...]
    lhs = (gb_ref[...].astype(F32) * conv).astype(BF16)
    _accumulate(ln, lhs, w_ref, alpha, step, n_steps)
    _finish_rows(ln, step, n_steps)


def _residual_ln_call(kernel, name, lead_specs, lead_args, hres, mods, layer, gate_idx, rot_fn,
                      ln_g, ln_b, next_mod, tm, n_steps, alpha, **kernel_kwargs):
    rows, d = hres.shape
    rc = tm // n_steps
    rb = _pick(LN_ROW_CHUNK, tm // OUT_SLOTS)
    assert rc * n_steps == tm and rc % 8 == 0 and tm % (OUT_SLOTS * rb) == 0, (tm, n_steps, rb)
    rot = rot_fn.mod_row(tm)
    in_specs = list(lead_specs) + [
        pl.BlockSpec((rc, d), lambda i, k: (i * n_steps + k, 0)),
        _mod_spec(layer, gate_idx, rot, d),
        _row_spec(d),
        _row_spec(d),
    ]
    args = list(lead_args) + [hres, mods, ln_g, ln_b]
    out_shape = [jax.ShapeDtypeStruct((rows, d), F32)]
    scratch = [pltpu.VMEM((tm, d), F32), pltpu.VMEM((OUT_SLOTS, rb, d), F32), pltpu.SemaphoreType.DMA((OUT_SLOTS,))]
    has_next = next_mod is not None
    if has_next:
        nl, nidx = next_mod
        in_specs += [_mod_spec(nl, nidx, rot, d), _mod_spec(nl, nidx + 1, rot, d)]
        args += [mods, mods]
        out_shape.append(jax.ShapeDtypeStruct((rows, d), BF16))
        scratch += [pltpu.VMEM((OUT_SLOTS, rb, d), BF16), pltpu.SemaphoreType.DMA((OUT_SLOTS,))]
    res = pl.pallas_call(
        functools.partial(kernel, alpha=alpha, n_steps=n_steps, has_next=has_next, **kernel_kwargs),
        out_shape=tuple(out_shape),
        grid=(rows // tm, n_steps),
        in_specs=in_specs,
        out_specs=tuple(pl.BlockSpec(memory_space=pl.ANY) for _ in out_shape),
        scratch_shapes=scratch,
        compiler_params=_cparams(("arbitrary", "arbitrary")),
        name=name,
    )(*args)
    return res if has_next else (res[0], None)


def _mlp(a, w1, w2, hres, mods, layer, rot_fn, ln_g, ln_b, next_mod, alpha):
    rows, d = a.shape
    d_ff = w1.shape[2]
    tm = _pick(LN_TILE_ROWS, rot_fn.tile_limit)
    tf = _pick(MLP_FF_TILE, d_ff)
    lead_specs = [
        pl.BlockSpec((tm, d), lambda i, k: (i, 0)),
        pl.BlockSpec((None, d, tf), lambda i, k: (layer, 0, k)),
        pl.BlockSpec((None, tf, d), lambda i, k: (layer, k, 0)),
    ]
    return _residual_ln_call(_mlp_kernel, "mlp", lead_specs, [a, w1, w2], hres, mods, layer, 5, rot_fn,
                             ln_g, ln_b, next_mod, tm, d_ff // tf, alpha)


def _plain_out(lhs, w, hres, mods, layer, rot_fn, ln_g, ln_b, next_mod, alpha):
    rows, kdim = lhs.shape
    d = w.shape[1]
    tm = _pick(LN_TILE_ROWS, rot_fn.tile_limit)
    tk = _pick(ATTN_OUT_K_TILE, kdim)
    lead_specs = [
        pl.BlockSpec((tm, tk), lambda i, k: (i, k)),
        pl.BlockSpec((tk, d), lambda i, k: (k, 0)),
    ]
    return _residual_ln_call(_plain_out_kernel, "attn_out", lead_specs, [lhs, w], hres, mods, layer, 2,
                             rot_fn, ln_g, ln_b, next_mod, tm, kdim // tk, alpha)


def _conv_out(gb, u, conv_w, conv_b, w, hres, mods, layer, rot_fn, ln_g, ln_b, next_mod, alpha, seq_len):
    rows, kdim = u.shape
    d = w.shape[1]
    tm = _pick(LN_TILE_ROWS, rot_fn.tile_limit)
    tk = _pick(CONV_OUT_K_TILE, kdim)
    hb = BF16_SUBLANES
    n_halo = rows // hb
    lead_specs = [
        pl.BlockSpec((tm, tk), lambda i, k: (i, k)),
        pl.BlockSpec((tm, tk), lambda i, k: (i, k)),
        pl.BlockSpec((hb, tk), lambda i, k: (jnp.maximum(i * (tm // hb) - 1, 0), k)),
        pl.BlockSpec((hb, tk), lambda i, k: (jnp.minimum((i + 1) * (tm // hb), n_halo - 1), k)),
        pl.BlockSpec((3, tk), lambda i, k: (0, k)),
        pl.BlockSpec((1, tk), lambda i, k: (0, k)),
        pl.BlockSpec((tk, d), lambda i, k: (k, 0)),
    ]
    return _residual_ln_call(_conv_out_kernel, "conv_out", lead_specs, [gb, u, u, u, conv_w, conv_b, w],
                             hres, mods, layer, 2, rot_fn, ln_g, ln_b, next_mod, tm, kdim // tk, alpha,
                             seq_len=seq_len)


QKV_ROW_TILE = 1024
QKV_COL_TILE = 512


def _rms_rope_heads(y, gain_ref, rope_refs, o_ref):
    tm, tn = y.shape
    if rope_refs is not None:
        cos_ref, sin_ref = rope_refs
        lane = lax.broadcasted_iota(jnp.int32, (tm, HEAD_DIM), 1)
        first_half = (lane % (HEAD_DIM // 2)) < (HEAD_DIM // 4)
    for h in range(tn // HEAD_DIM):
        sl = slice(h * HEAD_DIM, (h + 1) * HEAD_DIM)
        x = y[:, sl]
        x = x * lax.rsqrt(jnp.mean(x * x, axis=-1, keepdims=True) + RMS_EPS) * gain_ref[...]
        if rope_refs is not None:
            rot = jnp.where(first_half,
                            pltpu.roll(x, HEAD_DIM - HEAD_DIM // 4, axis=1),
                            pltpu.roll(x, HEAD_DIM // 4, axis=1))
            x = x * cos_ref[...] + rot * sin_ref[...]
        o_ref[:, sl] = x.astype(BF16)


def _qkv_kernel(*refs, n_q_tiles, n_k_tiles, rope):
    if rope:
        a_ref, w_ref, qg_ref, kg_ref, cos_ref, sin_ref, o_ref, ws, y0_ref, y1_ref = refs
        rope_refs = (cos_ref, sin_ref)
    else:
        a_ref, w_ref, qg_ref, kg_ref, o_ref, ws, y0_ref, y1_ref = refs
        rope_refs = None
    j, i = pl.program_id(0), pl.program_id(1)

    @pl.when(i == 0)
    def _():
        ws[...] = w_ref[...].astype(BF16)

    @pl.when((i == 0) & (j == 0))
    def _():
        y1_ref[...] = jnp.zeros_like(y1_ref)

    def plain(y):
        o_ref[...] = y.astype(BF16)

    kinds = [((j >= n_q_tiles) & (j < n_q_tiles + n_k_tiles), lambda y: _rms_rope_heads(y, kg_ref, rope_refs, o_ref)),
             (j >= n_q_tiles + n_k_tiles, plain)]
    if n_q_tiles:
        kinds.append((j < n_q_tiles, lambda y: _rms_rope_heads(y, qg_ref, rope_refs, o_ref)))
    for parity, (y_new, y_old) in enumerate([(y0_ref, y1_ref), (y1_ref, y0_ref)]):
        for is_kind, epilogue in kinds:
            @pl.when((i % 2 == parity) & is_kind)
            def _(y_new=y_new, y_old=y_old, epilogue=epilogue):
                y_new[...] = _dot(a_ref[...], ws[...])
                epilogue(y_old[...])


def _qkv(a, w_qkv, q_gain, k_gain, rope_tables, col0, q_dim, kv_dim, seq_len):
    rows, d = a.shape
    n_cols = w_qkv.shape[1] - col0
    tm = _pick(QKV_ROW_TILE, seq_len if rope_tables is not None else rows)
    tn = _pick(QKV_COL_TILE, kv_dim)
    assert col0 % tn == 0 and q_dim % tn == 0
    j0 = col0 // tn
    n_q_tiles = (q_dim - col0) // tn if col0 < q_dim else 0
    n_k_tiles = kv_dim // tn
    rope = rope_tables is not None
    n_row_tiles = rows // tm
    in_specs = [
        pl.BlockSpec((tm, d), lambda j, i: (jnp.minimum(i, n_row_tiles - 1), 0)),
        pl.BlockSpec((d, tn), lambda j, i: (0, j + j0)),
        pl.BlockSpec((1, HEAD_DIM), lambda j, i: (0, 0)),
        pl.BlockSpec((1, HEAD_DIM), lambda j, i: (0, 0)),
    ]
    args = [a, w_qkv, q_gain, k_gain]
    if rope:
        tiles_per_seq = seq_len // tm
        tab = pl.BlockSpec((tm, HEAD_DIM), lambda j, i: (jnp.maximum(i - 1, 0) % tiles_per_seq, 0))
        in_specs += [tab, tab]
        args += list(rope_tables)
    return pl.pallas_call(
        functools.partial(_qkv_kernel, n_q_tiles=n_q_tiles, n_k_tiles=n_k_tiles, rope=rope),
        out_shape=jax.ShapeDtypeStruct((rows, n_cols), BF16),
        grid=(n_cols // tn, n_row_tiles + 1),
        in_specs=in_specs,
        out_specs=pl.BlockSpec((tm, tn), lambda j, i: (jnp.maximum(i - 1, 0), j)),
        scratch_shapes=[pltpu.VMEM((d, tn), BF16), pltpu.VMEM((tm, tn), F32), pltpu.VMEM((tm, tn), F32)],
        compiler_params=_cparams(("arbitrary", "arbitrary")),
        name="qkv" if rope else "kv_ctx",
    )(*args)


def _rope_tables(seq_len):
    rows = seq_len // GRID_W
    row = jnp.repeat(jnp.arange(rows), GRID_W)
    col = jnp.tile(jnp.arange(GRID_W), rows)
    half = HEAD_DIM // 2
    inv_freq = ROPE_THETA ** (-jnp.arange(0, half, 2, dtype=F32) / half)

    def axis_angles(pos):
        ang = pos.astype(F32)[:, None] * inv_freq[None, :]
        return jnp.concatenate([ang, ang], axis=-1)

    ang = jnp.concatenate([axis_angles(row), axis_angles(col)], axis=-1)
    lane = jnp.arange(HEAD_DIM)
    sign = jnp.where((lane % half) < (HEAD_DIM // 4), -1.0, 1.0).astype(F32)
    return jnp.cos(ang), jnp.sin(ang) * sign[None, :]


_NT = (((1,), (1,)), ((), ()))
ATTN_KEY_CHUNK = 256
ATTN_Q_TILE = 128


_TN = (((0,), (0,)), ((), ()))
SUBLANES = 8
_DONE = object()


def _row_group_fold(x, op):
    n = x.shape[0] // SUBLANES
    return op(x.reshape(n, SUBLANES, x.shape[1]), axis=0)


def _score_tile(q_ref, kc_ref, kl_ref, s_ref, m_ref, key_chunk):
    ctx_len, seq_len = kc_ref.shape[0], kl_ref.shape[0]
    q = jnp.concatenate([q_ref[:, g * HEAD_DIM:(g + 1) * HEAD_DIM] for g in range(GQA_GROUP)], axis=0)
    segs = [(kc_ref, 0, ctx_len, 0)] + [(kl_ref, r, key_chunk, ctx_len + r) for r in range(0, seq_len, key_chunk)]
    m_rows = None
    for k_ref, r0, cnt, c0 in segs:
        st = lax.dot_general(k_ref[r0:r0 + cnt, :], q, _NT, preferred_element_type=F32)
        s_ref[c0:c0 + cnt, :] = st
        fold = _row_group_fold(st, jnp.max)
        m_rows = fold if m_rows is None else jnp.maximum(m_rows, fold)
        yield
    m_ref[...] = jnp.broadcast_to(jnp.max(m_rows, axis=0, keepdims=True), m_ref.shape)


def _finish_tile(s_ref, m_ref, vc_ref, vl_ref, o_ref, row0, scale, key_chunk):
    ctx_len, seq_len = vc_ref.shape[0], vl_ref.shape[0]
    rows = s_ref.shape[1]
    tq = rows // GQA_GROUP
    segs = [(vc_ref, 0, ctx_len, 0)] + [(vl_ref, r, key_chunk, ctx_len + r) for r in range(0, seq_len, key_chunk)]
    m = m_ref[0:1, :]
    l_rows = jnp.zeros((SUBLANES, rows), F32)
    o_t = jnp.zeros((HEAD_DIM, rows), F32)
    for v_ref, r0, cnt, c0 in segs:
        p_t = jnp.exp2((s_ref[c0:c0 + cnt, :] - m) * (scale * LOG2_E))
        l_rows = l_rows + _row_group_fold(p_t, jnp.sum)
        o_t = o_t + lax.dot_general(v_ref[r0:r0 + cnt, :], p_t.astype(BF16), _TN, preferred_element_type=F32)
        yield
    o = (o_t / jnp.sum(l_rows, axis=0, keepdims=True)).T
    for g in range(GQA_GROUP):
        o_ref[row0:row0 + tq, g * HEAD_DIM:(g + 1) * HEAD_DIM] = o[g * tq:(g + 1) * tq, :].astype(BF16)


def _attention_kernel(q_ref, kc_ref, kl_ref, vc_ref, vl_ref, o_ref, s0_ref, s1_ref, m0_ref, m1_ref,
                      *, scale, key_chunk):
    k = pl.program_id(0)

    @pl.when(k == 0)
    def _():
        s1_ref[...] = jnp.zeros_like(s1_ref)
        m1_ref[...] = jnp.zeros_like(m1_ref)

    def step(s_new, m_new, s_old, m_old):
        stages = [_finish_tile(s_old, m_old, vc_ref, vl_ref, o_ref, 0, scale, key_chunk),
                  _score_tile(q_ref, kc_ref, kl_ref, s_new, m_new, key_chunk)]
        while stages:
            stages = [stage for stage in stages if next(stage, _DONE) is not _DONE]

    @pl.when(k % 2 == 0)
    def _():
        step(s0_ref, m0_ref, s1_ref, m1_ref)

    @pl.when(k % 2 == 1)
    def _():
        step(s1_ref, m1_ref, s0_ref, m0_ref)


def _attention(qkv_lat, kv_ctx, bsz, seq_len, ctx_len, q_dim, kv_dim):
    n_kv = kv_dim // HEAD_DIM
    gw = GQA_GROUP * HEAD_DIM
    tq = _pick(ATTN_Q_TILE, seq_len)
    tiles = seq_len // tq
    n_tiles = bsz * n_kv * tiles
    k0 = q_dim // HEAD_DIM
    v0 = k0 + n_kv
    key_chunk = _pick(ATTN_KEY_CHUNK, seq_len)
    assert ctx_len % HEAD_DIM == 0 and key_chunk % HEAD_DIM == 0

    def tile_of(n):
        n = jnp.clip(n, 0, n_tiles - 1)
        return n // (n_kv * tiles), (n // tiles) % n_kv, n % tiles

    def at(offset, f):
        return lambda k: f(*tile_of(k + offset))

    rows = GQA_GROUP * tq
    return pl.pallas_call(
        functools.partial(_attention_kernel, scale=HEAD_DIM ** -0.5, key_chunk=key_chunk),
        out_shape=jax.ShapeDtypeStruct((bsz * seq_len, q_dim), BF16),
        grid=(n_tiles + 1,),
        in_specs=[
            pl.BlockSpec((tq, gw), at(0, lambda b, h, t: (b * tiles + t, h))),
            pl.BlockSpec((ctx_len, HEAD_DIM), at(0, lambda b, h, t: (b, h))),
            pl.BlockSpec((seq_len, HEAD_DIM), at(0, lambda b, h, t: (b, k0 + h))),
            pl.BlockSpec((ctx_len, HEAD_DIM), at(-1, lambda b, h, t: (b, n_kv + h))),
            pl.BlockSpec((seq_len, HEAD_DIM), at(-1, lambda b, h, t: (b, v0 + h))),
        ],
        out_specs=pl.BlockSpec((tq, gw), at(-1, lambda b, h, t: (b * tiles + t, h))),
        scratch_shapes=[pltpu.VMEM((ctx_len + seq_len, rows), F32), pltpu.VMEM((ctx_len + seq_len, rows), F32),
                        pltpu.VMEM((SUBLANES, rows), F32), pltpu.VMEM((SUBLANES, rows), F32)],
        compiler_params=_cparams(("arbitrary",)),
        name="attention",
    )(qkv_lat, kv_ctx, qkv_lat, kv_ctx, qkv_lat)


def kernel(x, c, ctx, c_ctx, ada_w, ada_b, ln_g, ln_b, mlp_w1, mlp_w2, conv_in_w, conv_w, conv_b, conv_out_w,
           attn_qkv_w, attn_q_gain, attn_k_gain, attn_out_w):
    bsz, seq_len, d = x.shape
    ctx_len = ctx.shape[1]
    depth = ada_w.shape[0]
    assert depth == 2 and bsz < COND_ROWS, "one conv-mixer layer followed by one attention layer"
    alpha = (2 * depth) ** 0.25
    kv_dim = (attn_qkv_w.shape[2] - d) // 2
    q_dim = d

    cond = jnp.zeros((COND_ROWS, d), F32).at[:bsz].set(c).at[bsz].set(c_ctx)
    mods = _adaln(cond, ada_w, ada_b).reshape(depth, COND_ROWS, N_MOD, 1, d)

    x_lat = x.reshape(bsz * seq_len, d)
    x_ctx = ctx.reshape(bsz * ctx_len, d)
    lat_rot = _RowMap(seq_len, lambda tm: (lambda i: i // (seq_len // tm)))
    ctx_rot = _RowMap(bsz * ctx_len, lambda tm: (lambda i: bsz))
    row = lambda v: v.reshape(1, -1)

    conv_in_wb = _to_bf16(conv_in_w)[0]

    def layer0(h, rot_fn, sub_len, final_next_mod, weights_bf16):
        a = _modulate(h, mods, 0, rot_fn, _pick(MODULATE_ROW_TILE, rot_fn.tile_limit))
        if weights_bf16 is None:
            gb, u, weights_bf16 = _conv_in(a, conv_in_wb, rot_fn.tile_limit,
                                           (mlp_w1, mlp_w2, conv_out_w, attn_out_w))
        else:
            gb, u, _ = _conv_in(a, conv_in_wb, rot_fn.tile_limit)
        w1b, w2b, conv_out_wb, _ = weights_bf16
        h1, a1 = _conv_out(gb, u, conv_w[0], row(conv_b[0]), conv_out_wb[0], h, mods, 0, rot_fn,
                           row(ln_g[0, 0]), row(ln_b[0, 0]), (0, 3), alpha, sub_len)
        h2, a2 = _mlp(a1, w1b, w2b, h1, mods, 0, rot_fn, row(ln_g[0, 1]), row(ln_b[0, 1]),
                      final_next_mod, alpha)
        return h2, a2, weights_bf16

    h_lat, a_lat, weights_bf16 = layer0(x_lat, lat_rot, seq_len, (1, 0), None)
    _, a_ctx, _ = layer0(x_ctx, ctx_rot, ctx_len, (1, 0), weights_bf16)
    w1b, w2b, _, attn_out_wb = weights_bf16

    cos, sin_signed = _rope_tables(seq_len)
    qg, kg = row(attn_q_gain[0]), row(attn_k_gain[0])
    qkv_lat = _qkv(a_lat, attn_qkv_w[0], qg, kg, (cos, sin_signed), 0, q_dim, kv_dim, seq_len)
    kv_ctx = _qkv(a_ctx, attn_qkv_w[0], qg, kg, None, q_dim, q_dim, kv_dim, ctx_len)
    o = _attention(qkv_lat, kv_ctx, bsz, seq_len, ctx_len, q_dim, kv_dim)
    h_lat, a_lat = _plain_out(o, attn_out_wb[0], h_lat, mods, 1, lat_rot, row(ln_g[1, 0]), row(ln_b[1, 0]),
                              (1, 3), alpha)
    h_lat, _ = _mlp(a_lat, w1b, w2b, h_lat, mods, 1, lat_rot, row(ln_g[1, 1]), row(ln_b[1, 1]),
                    None, alpha)
    return h_lat.reshape(bsz, seq_len, d)
```
